```python
import jax, jax.numpy as jnp
from jax import lax
import numpy as np

D_MODEL = 1024
BATCH = 4
SEQ = 8192
DEPTH = 1
DEC_BATCH = 128
DEC_SEQ = 8
PAST_LEN = 16384
PAGE_SIZE = 128

GLA_HEADS = 8
GLA_DK = 32
GLA_DV = 64
GLA_LOW_RANK = 16
GLA_GATE_NORM = 16.0
GLA_CHUNK = 64
SWA_HEADS = 8
SWA_KV_HEADS = 2
SWA_DIM = 64
WINDOW = 128
SWA_BLOCK = 128
MIX_WIDTH = GLA_HEADS * GLA_DV + SWA_HEADS * SWA_DIM
D_FF = -(-8 * D_MODEL // (3 * 256)) * 256
PLE_DIM = 256
EPS = 1e-6
IN_SPLITS = (GLA_HEADS * GLA_DK, GLA_HEADS * GLA_DK, GLA_HEADS * GLA_DV, GLA_HEADS * GLA_DV,
             GLA_LOW_RANK, SWA_HEADS * SWA_DIM, SWA_KV_HEADS * SWA_DIM, SWA_KV_HEADS * SWA_DIM)
D_IN = sum(IN_SPLITS)

kernel_name = "hymba_gla_swa_sink_alibi_decode_step"


def rmsnorm(x, w):
    xf = x.astype(jnp.float32)
    y = xf * lax.rsqrt(jnp.mean(xf * xf, axis=-1, keepdims=True) + EPS)
    return (y * w.astype(jnp.float32)).astype(x.dtype)


def alibi_slopes(n):
    return 2.0 ** (-8.0 * jnp.arange(1, n + 1, dtype=jnp.float32) / n)


def split_points():
    return [int(c) for c in np.cumsum(IN_SPLITS)[:-1]]


def sink_softmax(s, sink):
    m = jnp.maximum(jnp.max(s, axis=-1, keepdims=True), sink)
    e = jnp.exp(s - m)
    return e / (jnp.sum(e, axis=-1, keepdims=True) + jnp.exp(sink - m))


def gla_recurrence(q, k, v, log_a, s0):
    B, T, H, DK = q.shape
    DV = v.shape[-1]
    C = min(GLA_CHUNK, T)
    n = -(-T // C)
    pad = n * C - T

    def prep(a):
        a = jnp.pad(a.astype(jnp.float32), ((0, 0), (0, pad), (0, 0), (0, 0)))
        return a.reshape(B, n, C, H, a.shape[-1]).swapaxes(0, 1)

    mask = jnp.tril(jnp.ones((C, C), dtype=bool))[None, :, :, None, None]

    def step(S, inp):
        qc, kc, vc, gc = inp
        G = jnp.cumsum(gc, axis=1)
        diff = G[:, :, None] - G[:, None, :]
        decay = jnp.exp(jnp.where(mask, diff, -jnp.inf))
        A = jnp.einsum('bihd,bjhd,bijhd->bhij', qc, kc, decay)
        o = (jnp.einsum('bhij,bjhv->bihv', A, vc)
             + jnp.einsum('bihd,bhdv->bihv', qc * jnp.exp(G), S))
        G_last = G[:, -1]
        S = (S * jnp.exp(G_last)[..., None]
             + jnp.einsum('bjhd,bjhv->bhdv', kc * jnp.exp(G_last[:, None] - G), vc))
        return S, o

    S, o = lax.scan(step, s0.astype(jnp.float32), (prep(q), prep(k), prep(v), prep(log_a)))
    o = o.swapaxes(0, 1).reshape(B, n * C, H, DV)[:, :T]
    return o, S


def swa_prompt(q, k, v, sinks, slopes):
    B, T, Hq, D = q.shape
    Hkv = k.shape[2]
    G = Hq // Hkv
    L = SWA_BLOCK
    n = T // L
    qb = q.reshape(B, n, L, Hkv, G, D)

    def band(a):
        prev = jnp.pad(a, ((0, 0), (L, 0), (0, 0), (0, 0)))[:, :T]
        return jnp.concatenate([prev.reshape(B, n, L, Hkv, D), a.reshape(B, n, L, Hkv, D)], axis=2)

    kb, vb = band(k), band(v)
    blk = jnp.arange(n)[:, None]
    qpos = blk * L + jnp.arange(L)[None]
    kpos = blk * L - L + jnp.arange(2 * L)[None]
    dist = qpos[:, :, None] - kpos[:, None, :]
    valid = (dist >= 0) & (dist < WINDOW) & (kpos[:, None, :] >= 0)
    s = jnp.einsum('bnqhgd,bnkhd->bnhgqk', qb, kb).astype(jnp.float32) * (D ** -0.5)
    s = s - slopes.reshape(Hkv, G)[:, :, None, None] * dist[:, None, None].astype(jnp.float32)
    s = jnp.where(valid[:, None, None], s, -jnp.inf)
    pr = sink_softmax(s, sinks.reshape(Hkv, G)[:, :, None, None].astype(jnp.float32))
    o = jnp.einsum('bnhgqk,bnkhd->bnqhgd', pr.astype(v.dtype), vb)
    return o.reshape(B, T, Hq * D)


def swa_with_buffer(q, k, v, k_buf, v_buf, sinks, slopes):
    B, T, Hq, D = q.shape
    Hkv = k.shape[2]
    G = Hq // Hkv
    Wb = k_buf.shape[1]
    kk = jnp.concatenate([k_buf.astype(k.dtype), k], axis=1)
    vv = jnp.concatenate([v_buf.astype(v.dtype), v], axis=1)
    qpos = Wb + jnp.arange(T)
    kpos = jnp.arange(Wb + T)
    dist = qpos[:, None] - kpos[None, :]
    valid = (dist >= 0) & (dist < WINDOW)
    qg = q.reshape(B, T, Hkv, G, D)
    s = jnp.einsum('bqhgd,bkhd->bhgqk', qg, kk).astype(jnp.float32) * (D ** -0.5)
    s = s - slopes.reshape(Hkv, G)[:, :, None, None] * dist.astype(jnp.float32)
    s = jnp.where(valid, s, -jnp.inf)
    pr = sink_softmax(s, sinks.reshape(Hkv, G)[:, :, None, None].astype(jnp.float32))
    o = jnp.einsum('bhgqk,bkhd->bqhgd', pr.astype(vv.dtype), vv).reshape(B, T, Hq * D)
    return o, kk[:, -Wb:], vv[:, -Wb:]


def trunk_layer(x, p, gla_s0, k_buf, v_buf,
                w_pre_mix, w_in, gla_gate_up, gla_gate_b, gla_norm_w, swa_sinks, w_out,
                w_post_mix, w_pre_ffn, w_gate, w_up, w_down, w_post_ffn, w_ple_gate, w_ple_proj,
                is_prompt):
    B, T, _ = x.shape
    hn = rmsnorm(x, w_pre_mix)
    z = hn @ w_in
    q_g, k_g, v_g, og_g, lr_g, q_s, k_s, v_s = jnp.split(z, split_points(), axis=-1)

    log_a = jax.nn.log_sigmoid((lr_g @ gla_gate_up + gla_gate_b).astype(jnp.float32)) / GLA_GATE_NORM
    if gla_s0 is None:
        gla_s0 = jnp.zeros((B, GLA_HEADS, GLA_DK, GLA_DV), jnp.float32)
    o_g, s_new = gla_recurrence(q_g.reshape(B, T, GLA_HEADS, GLA_DK) * (GLA_DK ** -0.5),
                                k_g.reshape(B, T, GLA_HEADS, GLA_DK),
                                v_g.reshape(B, T, GLA_HEADS, GLA_DV),
                                log_a.reshape(B, T, GLA_HEADS, GLA_DK), gla_s0)
    o_g = rmsnorm(o_g.astype(x.dtype), gla_norm_w) * jax.nn.silu(og_g).reshape(B, T, GLA_HEADS, GLA_DV)
    o_g = o_g.reshape(B, T, GLA_HEADS * GLA_DV)

    slopes = alibi_slopes(SWA_HEADS)
    q_s = q_s.reshape(B, T, SWA_HEADS, SWA_DIM)
    k_s = k_s.reshape(B, T, SWA_KV_HEADS, SWA_DIM)
    v_s = v_s.reshape(B, T, SWA_KV_HEADS, SWA_DIM)
    if is_prompt:
        o_s = swa_prompt(q_s, k_s, v_s, swa_sinks, slopes)
        keep = min(WINDOW, T)
        k_new, v_new = k_s[:, T - keep:], v_s[:, T - keep:]
    else:
        o_s, k_new, v_new = swa_with_buffer(q_s, k_s, v_s, k_buf, v_buf, swa_sinks, slopes)
        s_new = s_new.astype(gla_s0.dtype)

    mix = jnp.concatenate([o_g, o_s.astype(x.dtype)], axis=-1) @ w_out
    h = x + rmsnorm(mix, w_post_mix)
    f = rmsnorm(h, w_pre_ffn)
    f = (jax.nn.silu(f @ w_gate) * (f @ w_up)) @ w_down
    h = h + rmsnorm(f, w_post_ffn)
    h = h + jax.nn.sigmoid(h @ w_ple_gate) * (p @ w_ple_proj)
    return h, s_new, k_new, v_new


def setup_inputs(seed: int = 0) -> dict:
    key = jax.random.key(seed)
    ks = jax.random.split(key, 24)
    f32 = jnp.float32

    def nrm(k, shape, scale):
        return jax.random.normal(k, shape, f32) * scale

    def gain(k, n):
        return 1.0 + 0.05 * jax.random.normal(k, (DEPTH, n), f32)

    w_buf = min(WINDOW, PAST_LEN)
    return {
        "x_prompt": nrm(ks[0], (BATCH, SEQ, D_MODEL), 1.0),
        "x_sample": nrm(ks[1], (DEC_BATCH, DEC_SEQ, D_MODEL), 1.0),
        "p_prompt": nrm(ks[2], (DEPTH, BATCH, SEQ, PLE_DIM), 1.0),
        "p_sample": nrm(ks[3], (DEPTH, DEC_BATCH, DEC_SEQ, PLE_DIM), 1.0),
        "state_gla": nrm(ks[4], (DEPTH, DEC_BATCH, GLA_HEADS, GLA_DK, GLA_DV), 0.5),
        "cache_swa_k": nrm(ks[5], (DEPTH, DEC_BATCH, w_buf, SWA_KV_HEADS, SWA_DIM), 1.0),
        "cache_swa_v": nrm(ks[6], (DEPTH, DEC_BATCH, w_buf, SWA_KV_HEADS, SWA_DIM), 1.0),
        "w_pre_mix": gain(ks[7], D_MODEL),
        "w_in": nrm(ks[8], (DEPTH, D_MODEL, D_IN), D_MODEL ** -0.5),
        "gla_gate_up": nrm(ks[9], (DEPTH, GLA_LOW_RANK, GLA_HEADS * GLA_DK), GLA_LOW_RANK ** -0.5),
        "gla_gate_b": nrm(ks[10], (DEPTH, GLA_HEADS * GLA_DK), 0.1),
        "gla_norm_w": gain(ks[11], GLA_DV),
        "swa_sinks": nrm(ks[12], (DEPTH, SWA_HEADS), 1.0),
        "w_out": nrm(ks[13], (DEPTH, MIX_WIDTH, D_MODEL), MIX_WIDTH ** -0.5),
        "w_post_mix": gain(ks[14], D_MODEL),
        "w_pre_ffn": gain(ks[15], D_MODEL),
        "w_gate": nrm(ks[16], (DEPTH, D_MODEL, D_FF), D_MODEL ** -0.5),
        "w_up": nrm(ks[17], (DEPTH, D_MODEL, D_FF), D_MODEL ** -0.5),
        "w_down": nrm(ks[18], (DEPTH, D_FF, D_MODEL), D_FF ** -0.5),
        "w_post_ffn": gain(ks[19], D_MODEL),
        "w_ple_gate": nrm(ks[20], (DEPTH, D_MODEL, D_MODEL), D_MODEL ** -0.5),
        "w_ple_proj": nrm(ks[21], (DEPTH, PLE_DIM, D_MODEL), PLE_DIM ** -0.5),
    }


def reference(x_prompt, x_sample, p_prompt, p_sample, state_gla, cache_swa_k, cache_swa_v,
              w_pre_mix, w_in, gla_gate_up, gla_gate_b, gla_norm_w, swa_sinks, w_out,
              w_post_mix, w_pre_ffn, w_gate, w_up, w_down, w_post_ffn, w_ple_gate, w_ple_proj):
    h_p, h_s = x_prompt, x_sample
    gla_p, k_p, v_p, gla_s, k_s, v_s = [], [], [], [], [], []
    for i in range(DEPTH):
        lw = (w_pre_mix[i], w_in[i], gla_gate_up[i], gla_gate_b[i], gla_norm_w[i], swa_sinks[i],
              w_out[i], w_post_mix[i], w_pre_ffn[i], w_gate[i], w_up[i], w_down[i], w_post_ffn[i],
              w_ple_gate[i], w_ple_proj[i])
        h_p, sg, kb, vb = trunk_layer(h_p, p_prompt[i], None, None, None, *lw, is_prompt=True)
        gla_p.append(sg); k_p.append(kb); v_p.append(vb)
        h_s, sg, kb, vb = trunk_layer(h_s, p_sample[i], state_gla[i], cache_swa_k[i], cache_swa_v[i],
                                      *lw, is_prompt=False)
        gla_s.append(sg); k_s.append(kb); v_s.append(vb)
    return (h_p, h_s, jnp.stack(gla_p), jnp.stack(k_p), jnp.stack(v_p),
            jnp.stack(gla_s), jnp.stack(k_s), jnp.stack(v_s))
```

```python
import functools

import numpy as np
import jax
import jax.numpy as jnp
from jax import lax
from jax.experimental import pallas as pl
from jax.experimental.pallas import tpu as pltpu

F32 = jnp.float32
BF16 = jnp.bfloat16

D_MODEL = 1024
GLA_HEADS = 8
GLA_DK = 32
GLA_DV = 64
GLA_LOW_RANK = 16
GLA_GATE_NORM = 16.0
SWA_HEADS = 8
SWA_KV_HEADS = 2
SWA_DIM = 64
WINDOW = 128
D_FF = 2816
PLE_DIM = 256
EPS = 1e-6

QK_W = GLA_HEADS * GLA_DK
V_W = GLA_HEADS * GLA_DV
SQ_W = SWA_HEADS * SWA_DIM
SKV_W = SWA_KV_HEADS * SWA_DIM
LR_PAD = 128
IN_SPLITS = (QK_W, QK_W, V_W, V_W, GLA_LOW_RANK, SQ_W, SKV_W, SKV_W)

C_QG, C_KG, C_VG, C_OG, C_QS, C_KS, C_VS, C_LR = 0, 256, 512, 1024, 1536, 2048, 2176, 2304
D_IN_P = C_LR + LR_PAD

SWA_HEAD_ORDER = (0, 4, 1, 5, 2, 6, 3, 7)

GLA_CHUNK = 64
GLA_SAFE_LOG_DECAY = -60.0 / GLA_CHUNK

VMEM_LIMIT = 56 * 1024 * 1024


def _sigmoid(x):
    return 1.0 / (1.0 + jnp.exp(-x))


def _rms(x, w):
    ms = jnp.mean(x * x, axis=-1, keepdims=True)
    return x * lax.rsqrt(ms + EPS) * w


def _dot(a, b):
    return jnp.dot(a, b, preferred_element_type=F32)


def _dot_nt(a, b):
    return lax.dot_general(a, b, (((1,), (1,)), ((), ())), preferred_element_type=F32)


def _dot_tn(a, b):
    return lax.dot_general(a, b, (((0,), (0,)), ((), ())), preferred_element_type=F32)


def _in_proj_kernel(x_ref, wpre_ref, win_ref, gup_ref, gb_ref,
                    qg_ref, kg_ref, vg_ref, og_ref, la_ref, qs_ref, ks_ref, vs_ref):
    hn = _rms(x_ref[...], wpre_ref[...]).astype(BF16)

    def proj(lo, hi):
        return _dot(hn, win_ref[:, lo:hi])

    qg_ref[...] = proj(C_QG, C_KG) * (GLA_DK ** -0.5)
    kg_ref[...] = proj(C_KG, C_VG)
    vg_ref[...] = proj(C_VG, C_OG)
    og_ref[...] = proj(C_OG, C_QS)
    qs_ref[...] = proj(C_QS, C_KS) * (SWA_DIM ** -0.5)
    ks_ref[...] = proj(C_KS, C_VS)
    vs_ref[...] = proj(C_VS, C_LR)
    lr = proj(C_LR, D_IN_P).astype(BF16)
    pre = _dot(lr, gup_ref[...]) + gb_ref[...]
    log_sig = jnp.minimum(pre, 0.0) - jnp.log1p(jnp.exp(-jnp.abs(pre)))
    la_ref[...] = log_sig * (1.0 / GLA_GATE_NORM)


def _in_proj(x2d, w_pre, w_in_p, gup_p, gate_b, tm):
    n = x2d.shape[0]
    widths = (QK_W, QK_W, V_W, V_W, QK_W, SQ_W, SKV_W, SKV_W)
    const = lambda a: pl.BlockSpec(a.shape, lambda i: (0, 0))
    return pl.pallas_call(
        _in_proj_kernel,
        grid=(n // tm,),
        in_specs=[pl.BlockSpec((tm, D_MODEL), lambda i: (i, 0)),
                  const(w_pre), const(w_in_p), const(gup_p), const(gate_b)],
        out_specs=[pl.BlockSpec((tm, w), lambda i: (i, 0)) for w in widths],
        out_shape=[jax.ShapeDtypeStruct((n, w), F32) for w in widths],
        compiler_params=pltpu.CompilerParams(
            dimension_semantics=("arbitrary",), vmem_limit_bytes=VMEM_LIMIT),
        name="in_proj",
    )(x2d, w_pre, w_in_p, gup_p, gate_b)


def _seg_cumsum(la, tri):
    a1 = la.astype(BF16)
    r1 = la - a1.astype(F32)
    a2 = r1.astype(BF16)
    a3 = (r1 - a2.astype(F32)).astype(BF16)
    g = _dot(tri, jnp.concatenate([a1, a2, a3], axis=1))
    return g[:, :QK_W] + g[:, QK_W:2 * QK_W] + g[:, 2 * QK_W:]


def _head_norm_gate(o, og, nw, e64):
    o2 = o * o
    hi = o2.astype(BF16)
    lo = (o2 - hi.astype(F32)).astype(BF16)
    ms = _dot(hi, e64) + _dot(lo, e64)
    return o * lax.rsqrt(ms + EPS) * nw * (og * _sigmoid(og))


def _block_diag_rows(x, head_shift):
    head = lax.broadcasted_iota(jnp.int32, x.shape, 1) >> head_shift
    return jnp.concatenate([jnp.where(head == h, x, 0.0) for h in range(GLA_HEADS)], axis=0)


def _fold_state(y):
    head = lax.broadcasted_iota(jnp.int32, (GLA_DV, QK_W), 1) >> 5
    acc = jnp.zeros((GLA_DV, QK_W), F32)
    for h in range(GLA_HEADS):
        acc = acc + jnp.where(head == h, y[GLA_DV * h:GLA_DV * (h + 1), :], 0.0)
    return acc


def _state_block_mask():
    r = lax.broadcasted_iota(jnp.int32, (V_W, QK_W), 0) >> 6
    c = lax.broadcasted_iota(jnp.int32, (V_W, QK_W), 1) >> 5
    return r == c


def _gla_prompt_kernel(q_ref, k_ref, v_ref, la_ref, og_ref, nw_ref, tri_ref, e_ref, e64_ref,
                       o_ref, st_ref, sbd, obuf, gbuf, *, nblk, tb):
    C = GLA_CHUNK
    c = pl.program_id(1)

    @pl.when(c == 0)
    def _():
        sbd[...] = jnp.zeros_like(sbd)

    safe = jnp.min(la_ref[...]) >= GLA_SAFE_LOG_DECAY
    bdmask = _state_block_mask()

    def chunk(ci, carry):
        r = pl.multiple_of(ci * C, C)
        q = q_ref[pl.ds(r, C), :]
        k = k_ref[pl.ds(r, C), :]
        v = v_ref[pl.ds(r, C), :]
        g = _seg_cumsum(la_ref[pl.ds(r, C), :], tri_ref[...])
        g_last = g[C - 1:C, :]
        qe = (q * jnp.exp(g)).astype(BF16)
        obuf[pl.ds(r, C), :] = _dot_nt(qe, sbd[...].astype(BF16))

        @pl.when(safe)
        def _():
            kbd = _block_diag_rows(k * jnp.exp(-g), 5).astype(BF16)
            a = _dot_nt(qe, kbd)
            row = lax.broadcasted_iota(jnp.int32, a.shape, 0)
            col = lax.broadcasted_iota(jnp.int32, a.shape, 1) & (C - 1)
            a = jnp.where(col <= row, a, 0.0).astype(BF16)
            vbd = _block_diag_rows(v, 6).astype(BF16)
            obuf[pl.ds(r, C), :] += _dot(a, vbd)

        @pl.when(jnp.logical_not(safe))
        def _():
            gbuf[...] = g
            row = lax.broadcasted_iota(jnp.int32, (C, QK_W), 0)

            def key(j, acc):
                kj = k_ref[pl.ds(r + j, 1), :]
                vj = v_ref[pl.ds(r + j, 1), :]
                gj = gbuf[pl.ds(j, 1), :]
                m = row >= j
                dec = jnp.where(m, jnp.exp(jnp.where(m, g - gj, 0.0)), 0.0)
                p = (q * kj * dec).astype(BF16)
                return acc + _dot(p, e_ref[...]) * vj

            obuf[pl.ds(r, C), :] += lax.fori_loop(0, C, key, jnp.zeros((C, V_W), F32))

        kt = (k * jnp.exp(g_last - g)).astype(BF16)
        upd = _dot_tn(v.astype(BF16), kt)
        sbd[...] = sbd[...] * jnp.exp(g_last) + jnp.where(bdmask, upd, 0.0)
        return carry

    lax.fori_loop(0, tb // C, chunk, 0)
    o_ref[...] = _head_norm_gate(obuf[...], og_ref[...], nw_ref[...], e64_ref[...])

    @pl.when(c == nblk - 1)
    def _():
        st_ref[0] = _fold_state(sbd[...])


def _gla_prompt(qg, kg, vg, la, og, nw, tri, e, e64, batch, seq, tb):
    nblk = seq // tb
    row = lambda w: pl.BlockSpec((tb, w), lambda b, c: (b * nblk + c, 0))
    const = lambda a: pl.BlockSpec(a.shape, lambda b, c: (0, 0))
    return pl.pallas_call(
        functools.partial(_gla_prompt_kernel, nblk=nblk, tb=tb),
        grid=(batch, nblk),
        in_specs=[row(QK_W), row(QK_W), row(V_W), row(QK_W), row(V_W),
                  const(nw), const(tri), const(e), const(e64)],
        out_specs=[row(V_W), pl.BlockSpec((1, GLA_DV, QK_W), lambda b, c: (b, 0, 0))],
        out_shape=[jax.ShapeDtypeStruct((batch * seq, V_W), F32),
                   jax.ShapeDtypeStruct((batch, GLA_DV, QK_W), F32)],
        scratch_shapes=[pltpu.VMEM((V_W, QK_W), F32), pltpu.VMEM((tb, V_W), F32),
                        pltpu.VMEM((GLA_CHUNK, QK_W), F32)],
        compiler_params=pltpu.CompilerParams(
            dimension_semantics=("arbitrary", "arbitrary"), vmem_limit_bytes=VMEM_LIMIT),
        name="gla_prompt",
    )(qg, kg, vg, la, og, nw, tri, e, e64)


GLA_SAMPLE_SEQS = 8


def _gla_sample_kernel(q_ref, k_ref, v_ref, la_ref, og_ref, nw_ref, tri_ref, e_ref, e64_ref,
                       s0_ref, o_ref, s1_ref, *, t):
    ns = GLA_SAMPLE_SEQS
    rows = ns * t
    q = q_ref[...]
    k = k_ref[...]
    v = v_ref[...]
    g = _seg_cumsum(la_ref[...], tri_ref[...])
    g3 = g.reshape(ns, t, QK_W)
    q3 = q.reshape(ns, t, QK_W)
    k3 = k.reshape(ns, t, QK_W)
    v3 = v.reshape(ns, t, V_W)
    irow = lax.broadcasted_iota(jnp.int32, (ns, t, QK_W), 1)
    acc = jnp.zeros((rows, V_W), F32)
    for j in range(t):
        m = irow >= j
        dec = jnp.where(m, jnp.exp(jnp.where(m, g3 - g3[:, j:j + 1, :], 0.0)), 0.0)
        p = (q3 * k3[:, j:j + 1, :] * dec).reshape(rows, QK_W).astype(BF16)
        vj = jnp.broadcast_to(v3[:, j:j + 1, :], (ns, t, V_W)).reshape(rows, V_W)
        acc = acc + _dot(p, e_ref[...]) * vj

    qe = (q * jnp.exp(g)).astype(BF16)
    g_last = jnp.broadcast_to(g3[:, t - 1:t, :], (ns, t, QK_W)).reshape(rows, QK_W)
    kt = k * jnp.exp(g_last - g)
    v_bf = v.astype(BF16)
    seq_o = lax.broadcasted_iota(jnp.int32, (rows, V_W), 0) >> 3
    seq_k = lax.broadcasted_iota(jnp.int32, (rows, QK_W), 0) >> 3
    bdmask = _state_block_mask()
    for s in range(ns):
        s0 = s0_ref[s]
        sbd = jnp.where(bdmask, jnp.concatenate([s0] * GLA_HEADS, axis=0), 0.0).astype(BF16)
        acc = acc + jnp.where(seq_o == s, _dot_nt(qe, sbd), 0.0)
        kts = jnp.where(seq_k == s, kt, 0.0).astype(BF16)
        s1_ref[s] = s0 * jnp.exp(g[t * s + t - 1:t * s + t, :]) + _fold_state(_dot_tn(v_bf, kts))

    o_ref[...] = _head_norm_gate(acc, og_ref[...], nw_ref[...], e64_ref[...])


def _gla_sample(qg, kg, vg, la, og, nw, tri, e, e64, s0t, t):
    n = qg.shape[0]
    ns = GLA_SAMPLE_SEQS
    rows = ns * t
    row = lambda w: pl.BlockSpec((rows, w), lambda i: (i, 0))
    const = lambda a: pl.BlockSpec(a.shape, lambda i: (0, 0))
    st = pl.BlockSpec((ns, GLA_DV, QK_W), lambda i: (i, 0, 0))
    return pl.pallas_call(
        functools.partial(_gla_sample_kernel, t=t),
        grid=(n // rows,),
        in_specs=[row(QK_W), row(QK_W), row(V_W), row(QK_W), row(V_W),
                  const(nw), const(tri), const(e), const(e64), st],
        out_specs=[row(V_W), st],
        out_shape=[jax.ShapeDtypeStruct((n, V_W), F32),
                   jax.ShapeDtypeStruct(s0t.shape, F32)],
        compiler_params=pltpu.CompilerParams(
            dimension_semantics=("arbitrary",), vmem_limit_bytes=VMEM_LIMIT),
        name="gla_sample",
    )(qg, kg, vg, la, og, nw, tri, e, e64, s0t)


def _swa_attend(q, kband, vband, bias, sink):
    r = q.shape[0]
    low = lax.broadcasted_iota(jnp.int32, (r, SKV_W), 1) < SWA_DIM
    pieces = []
    for p in range(4):
        blk = q[:, SKV_W * p:SKV_W * (p + 1)]
        pieces.append(jnp.where(low, blk, 0.0))
        pieces.append(jnp.where(low, 0.0, blk))
    lhs = jnp.concatenate(pieces, axis=0).astype(BF16)
    s = _dot_nt(lhs, kband.astype(BF16)) + bias
    m = jnp.maximum(jnp.max(s, axis=-1, keepdims=True), sink)
    e = jnp.exp(s - m)
    den = jnp.sum(e, axis=-1, keepdims=True) + jnp.exp(sink - m)
    pv = _dot((e / den).astype(BF16), vband.astype(BF16))
    outs = [jnp.where(low, pv[2 * p * r:(2 * p + 1) * r], pv[(2 * p + 1) * r:(2 * p + 2) * r])
            for p in range(4)]
    return jnp.concatenate(outs, axis=1)


def _swa_prompt_kernel(q_ref, kp_ref, kc_ref, vp_ref, vc_ref, bias_ref, sink_ref, o_ref):
    blk = pl.program_id(1)
    kband = jnp.concatenate([kp_ref[...], kc_ref[...]], axis=0)
    vband = jnp.concatenate([vp_ref[...], vc_ref[...]], axis=0)
    key = lax.broadcasted_iota(jnp.int32, bias_ref.shape, 1)
    bias = jnp.where(jnp.logical_or(blk > 0, key >= WINDOW), bias_ref[...], -jnp.inf)
    o_ref[...] = _swa_attend(q_ref[...], kband, vband, bias, sink_ref[...])


def _swa_prompt(qs, ks, vs, bias, sink, batch, seq):
    L = WINDOW
    nb = seq // L
    cur = lambda w: pl.BlockSpec((L, w), lambda b, i: (b * nb + i, 0))
    prev = lambda w: pl.BlockSpec((L, w), lambda b, i: (b * nb + jnp.maximum(i - 1, 0), 0))
    const = lambda a: pl.BlockSpec(a.shape, lambda b, i: (0, 0))
    return pl.pallas_call(
        _swa_prompt_kernel,
        grid=(batch, nb),
        in_specs=[cur(SQ_W), prev(SKV_W), cur(SKV_W), prev(SKV_W), cur(SKV_W),
                  const(bias), const(sink)],
        out_specs=cur(SQ_W),
        out_shape=jax.ShapeDtypeStruct((batch * seq, SQ_W), F32),
        compiler_params=pltpu.CompilerParams(
            dimension_semantics=("arbitrary", "arbitrary"), vmem_limit_bytes=VMEM_LIMIT),
        name="swa_prompt",
    )(qs, ks, ks, vs, vs, bias, sink)


SWA_SAMPLE_SEQS = 16


def _swa_sample_kernel(q_ref, kn_ref, vn_ref, kc_ref, vc_ref, bias_ref, sink_ref,
                       o_ref, ko_ref, vo_ref, *, t):
    wb = kc_ref.shape[1]
    pad = jnp.zeros((WINDOW - t, SKV_W), F32)

    def seq(s, carry):
        kn = kn_ref[s]
        vn = vn_ref[s]
        kc = kc_ref[s]
        vc = vc_ref[s]
        kband = jnp.concatenate([kc, kn, pad], axis=0)
        vband = jnp.concatenate([vc, vn, pad], axis=0)
        o_ref[s] = _swa_attend(q_ref[s], kband, vband, bias_ref[...], sink_ref[...])
        ko_ref[s] = jnp.concatenate([kc[t:wb], kn], axis=0)
        vo_ref[s] = jnp.concatenate([vc[t:wb], vn], axis=0)
        return carry

    lax.fori_loop(0, SWA_SAMPLE_SEQS, seq, 0)


def _swa_sample(qs3, kn3, vn3, kc3, vc3, bias, sink, t):
    nseq = qs3.shape[0]
    ns = SWA_SAMPLE_SEQS
    blk = lambda a: pl.BlockSpec((ns,) + a.shape[1:], lambda i: (i, 0, 0))
    const = lambda a: pl.BlockSpec(a.shape, lambda i: (0, 0))
    return pl.pallas_call(
        functools.partial(_swa_sample_kernel, t=t),
        grid=(nseq // ns,),
        in_specs=[blk(qs3), blk(kn3), blk(vn3), blk(kc3), blk(vc3), const(bias), const(sink)],
        out_specs=[blk(qs3), blk(kc3), blk(vc3)],
        out_shape=[jax.ShapeDtypeStruct(qs3.shape, F32),
                   jax.ShapeDtypeStruct(kc3.shape, F32),
                   jax.ShapeDtypeStruct(vc3.shape, F32)],
        compiler_params=pltpu.CompilerParams(
            dimension_semantics=("arbitrary",), vmem_limit_bytes=VMEM_LIMIT),
        name="swa_sample",
    )(qs3, kn3, vn3, kc3, vc3, bias, sink)


def _swa_bias_table(r, sample):
    rows = np.arange(8 * r)
    piece = rows // r
    head = piece // 2 + 4 * (piece % 2)
    i = (rows % r)[:, None]
    slope = (2.0 ** -(head + 1.0))[:, None]
    key = np.arange(2 * WINDOW)[None, :]
    if sample:
        dist = np.where(key < WINDOW, WINDOW + i - key, i - (key - WINDOW))
        valid = (dist >= 0) & (dist < WINDOW) & (key < WINDOW + r)
    else:
        dist = i + WINDOW - key
        valid = (dist >= 0) & (dist < WINDOW)
    bias = np.where(valid, -slope * dist, -np.inf).astype(np.float32)
    return bias, head


def _post_kernel(og_ref, os_ref, x_ref, p_ref, wog_ref, wos_ref, wpm_ref, wpf_ref, wg_ref,
                 wu_ref, wd_ref, wpo_ref, wpg_ref, wpp_ref, y_ref):
    mix = _dot(og_ref[...].astype(BF16), wog_ref[...]) + _dot(os_ref[...].astype(BF16), wos_ref[...])
    h = x_ref[...] + _rms(mix, wpm_ref[...])
    f = _rms(h, wpf_ref[...]).astype(BF16)
    gate = _dot(f, wg_ref[...])
    up = _dot(f, wu_ref[...])
    act = (gate * _sigmoid(gate) * up).astype(BF16)
    h = h + _rms(_dot(act, wd_ref[...]), wpo_ref[...])
    ple = _sigmoid(_dot(h.astype(BF16), wpg_ref[...])) * _dot(p_ref[...].astype(BF16), wpp_ref[...])
    y_ref[...] = h + ple


def _post(og, osw, x2d, p2d, weights, tm):
    n = x2d.shape[0]
    row = lambda w: pl.BlockSpec((tm, w), lambda i: (i, 0))
    const = lambda a: pl.BlockSpec(a.shape, lambda i: (0, 0), pipeline_mode=pl.Buffered(1))
    return pl.pallas_call(
        _post_kernel,
        grid=(n // tm,),
        in_specs=[row(V_W), row(SQ_W), row(D_MODEL), row(PLE_DIM)] + [const(w) for w in weights],
        out_specs=row(D_MODEL),
        out_shape=jax.ShapeDtypeStruct((n, D_MODEL), F32),
        compiler_params=pltpu.CompilerParams(
            dimension_semantics=("arbitrary",), vmem_limit_bytes=VMEM_LIMIT),
        name="post",
    )(og, osw, x2d, p2d, *weights)


def _gla_constants(t_rows, seg):
    i = np.arange(t_rows)
    tri = ((i[:, None] >= i[None, :]) & (i[:, None] // seg == i[None, :] // seg))
    e = (np.arange(QK_W)[:, None] // GLA_DK == np.arange(V_W)[None, :] // GLA_DV)
    e64 = (np.arange(V_W)[:, None] // GLA_DV == np.arange(V_W)[None, :] // GLA_DV) / GLA_DV
    return (jnp.asarray(tri, BF16), jnp.asarray(e, BF16), jnp.asarray(e64, BF16))


def _state_from_t(st):
    b = st.shape[0]
    return st.reshape(b, GLA_DV, GLA_HEADS, GLA_DK).transpose(0, 2, 3, 1)


def _state_to_t(s):
    b = s.shape[0]
    return s.transpose(0, 3, 1, 2).reshape(b, GLA_DV, QK_W)


def kernel(x_prompt, x_sample, p_prompt, p_sample, state_gla, cache_swa_k, cache_swa_v,
           w_pre_mix, w_in, gla_gate_up, gla_gate_b, gla_norm_w, swa_sinks, w_out,
           w_post_mix, w_pre_ffn, w_gate, w_up, w_down, w_post_ffn, w_ple_gate, w_ple_proj):
    assert w_in.shape[0] == 1, "one trunk layer"
    batch, seq, _ = x_prompt.shape
    dec_batch, dec_seq, _ = x_sample.shape
    assert seq % 512 == 0 and (dec_batch * dec_seq) % 512 == 0
    assert dec_seq == 8, "a sample sequence is one 8-row sublane group"
    assert dec_batch % SWA_SAMPLE_SEQS == 0 and dec_batch % GLA_SAMPLE_SEQS == 0
    assert cache_swa_k.shape[2] == WINDOW
    order = np.asarray(SWA_HEAD_ORDER)

    parts = jnp.split(w_in[0], np.cumsum(IN_SPLITS)[:-1].tolist(), axis=1)
    w_qg, w_kg, w_vg, w_og, w_lr, w_qs, w_ks, w_vs = parts
    w_qs = w_qs.reshape(D_MODEL, SWA_HEADS, SWA_DIM)[:, order, :].reshape(D_MODEL, SQ_W)
    w_lr = jnp.pad(w_lr, ((0, 0), (0, LR_PAD - GLA_LOW_RANK)))
    w_in_p = jnp.concatenate([w_qg, w_kg, w_vg, w_og, w_qs, w_ks, w_vs, w_lr], axis=1).astype(BF16)
    gup_p = jnp.pad(gla_gate_up[0], ((0, LR_PAD - GLA_LOW_RANK), (0, 0))).astype(BF16)
    gate_b = gla_gate_b[0].reshape(1, QK_W)
    w_pre = w_pre_mix[0].reshape(1, D_MODEL)
    norm_w = jnp.tile(gla_norm_w[0], GLA_HEADS).reshape(1, V_W)
    w_out_g = w_out[0, :V_W].astype(BF16)
    w_out_s = (w_out[0, V_W:].reshape(SWA_HEADS, SWA_DIM, D_MODEL)[order]
               .reshape(SQ_W, D_MODEL).astype(BF16))
    post_w = (w_out_g, w_out_s, w_post_mix[0].reshape(1, D_MODEL), w_pre_ffn[0].reshape(1, D_MODEL),
              w_gate[0].astype(BF16), w_up[0].astype(BF16), w_down[0].astype(BF16),
              w_post_ffn[0].reshape(1, D_MODEL), w_ple_gate[0].astype(BF16),
              w_ple_proj[0].astype(BF16))

    xp = x_prompt.reshape(batch * seq, D_MODEL)
    qg, kg, vg, og, la, qs, ks, vs = _in_proj(xp, w_pre, w_in_p, gup_p, gate_b, tm=512)
    tri, e, e64 = _gla_constants(GLA_CHUNK, GLA_CHUNK)
    o_g, st_p = _gla_prompt(qg, kg, vg, la, og, norm_w, tri, e, e64, batch, seq, tb=512)
    bias_np, head_np = _swa_bias_table(WINDOW, sample=False)
    sink_p = swa_sinks[0][head_np].reshape(-1, 1)
    o_s = _swa_prompt(qs, ks, vs, jnp.asarray(bias_np), sink_p, batch, seq)
    y_p = _post(o_g, o_s, xp, p_prompt[0].reshape(batch * seq, PLE_DIM), post_w, tm=256)
    keep = min(WINDOW, seq)
    k_p = ks.reshape(batch, seq, SWA_KV_HEADS, SWA_DIM)[:, seq - keep:]
    v_p = vs.reshape(batch, seq, SWA_KV_HEADS, SWA_DIM)[:, seq - keep:]

    n_s = dec_batch * dec_seq
    xs = x_sample.reshape(n_s, D_MODEL)
    qg, kg, vg, og, la, qs, ks, vs = _in_proj(xs, w_pre, w_in_p, gup_p, gate_b, tm=512)
    tri_s, _, _ = _gla_constants(GLA_SAMPLE_SEQS * dec_seq, dec_seq)
    o_g, st_s = _gla_sample(qg, kg, vg, la, og, norm_w, tri_s, e, e64,
                            _state_to_t(state_gla[0]), dec_seq)
    bias_np, head_np = _swa_bias_table(dec_seq, sample=True)
    sink_s = swa_sinks[0][head_np].reshape(-1, 1)
    wb = cache_swa_k.shape[2]
    o_s3, k_s, v_s = _swa_sample(
        qs.reshape(dec_batch, dec_seq, SQ_W), ks.reshape(dec_batch, dec_seq, SKV_W),
        vs.reshape(dec_batch, dec_seq, SKV_W), cache_swa_k[0].reshape(dec_batch, wb, SKV_W),
        cache_swa_v[0].reshape(dec_batch, wb, SKV_W), jnp.asarray(bias_np), sink_s, dec_seq)
    y_s = _post(o_g, o_s3.reshape(n_s, SQ_W), xs, p_sample[0].reshape(n_s, PLE_DIM), post_w, tm=256)

    return (y_p.reshape(batch, seq, D_MODEL),
            y_s.reshape(dec_batch, dec_seq, D_MODEL),
            _state_from_t(st_p)[None],
            k_p[None], v_p[None],
            _state_from_t(st_s).astype(state_gla.dtype)[None],
            k_s.reshape(dec_batch, wb, SWA_KV_HEADS, SWA_DIM)[None],
            v_s.reshape(dec_batch, wb, SWA_KV_HEADS, SWA_DIM)[None])
```

```python
import functools

import numpy as np
import jax
import jax.numpy as jnp
from jax import lax
from jax.experimental import pallas as pl
from jax.experimental.pallas import tpu as pltpu

F32 = jnp.float32
BF16 = jnp.bfloat16

D_MODEL = 1024
GLA_HEADS = 8
GLA_DK = 32
GLA_DV = 64
GLA_LOW_RANK = 16
GLA_GATE_NORM = 16.0
SWA_HEADS = 8
SWA_KV_HEADS = 2
SWA_DIM = 64
WINDOW = 128
D_FF = 2816
PLE_DIM = 256
EPS = 1e-6

QK_W = GLA_HEADS * GLA_DK
V_W = GLA_HEADS * GLA_DV
SQ_W = SWA_HEADS * SWA_DIM
SKV_W = SWA_KV_HEADS * SWA_DIM
LR_PAD = 128
IN_SPLITS = (QK_W, QK_W, V_W, V_W, GLA_LOW_RANK, SQ_W, SKV_W, SKV_W)

C_QG, C_KG, C_VG, C_OG, C_QS, C_KS, C_VS, C_LR = 0, 256, 512, 1024, 1536, 2048, 2176, 2304
D_IN_P = C_LR + LR_PAD

SWA_HEAD_ORDER = (0, 4, 1, 5, 2, 6, 3, 7)

GLA_CHUNK = 128
GLA_MID = GLA_CHUNK // 2 - 1
GLA_SAFE_LOG_DECAY = -60.0 / (GLA_CHUNK // 2)

VMEM_LIMIT = 56 * 1024 * 1024


def _sigmoid(x):
    return 1.0 / (1.0 + jnp.exp(-x))


def _rms(x, w):
    ms = jnp.mean(x * x, axis=-1, keepdims=True)
    return x * lax.rsqrt(ms + EPS) * w


def _dot(a, b):
    return jnp.dot(a, b, preferred_element_type=F32)


def _dot_nt(a, b):
    return lax.dot_general(a, b, (((1,), (1,)), ((), ())), preferred_element_type=F32)


def _dot_tn(a, b):
    return lax.dot_general(a, b, (((0,), (0,)), ((), ())), preferred_element_type=F32)


def _in_proj_kernel(x_ref, wpre_ref, win_ref, gup_ref, gb_ref,
                    qg_ref, kg_ref, vg_ref, og_ref, la_ref, qs_ref, ks_ref, vs_ref):
    hn = _rms(x_ref[...], wpre_ref[...]).astype(BF16)

    def proj(lo, hi):
        return _dot(hn, win_ref[:, lo:hi])

    qg_ref[...] = (proj(C_QG, C_KG) * (GLA_DK ** -0.5)).astype(qg_ref.dtype)
    kg_ref[...] = proj(C_KG, C_VG).astype(kg_ref.dtype)
    vg_ref[...] = proj(C_VG, C_OG).astype(vg_ref.dtype)
    og_ref[...] = proj(C_OG, C_QS).astype(og_ref.dtype)
    qs_ref[...] = (proj(C_QS, C_KS) * (SWA_DIM ** -0.5)).astype(qs_ref.dtype)
    ks_ref[...] = proj(C_KS, C_VS)
    vs_ref[...] = proj(C_VS, C_LR)
    lr = proj(C_LR, D_IN_P).astype(BF16)
    pre = _dot(lr, gup_ref[...]) + gb_ref[...]
    log_sig = jnp.minimum(pre, 0.0) - jnp.log1p(jnp.exp(-jnp.abs(pre)))
    la_ref[...] = log_sig * (1.0 / GLA_GATE_NORM)


def _in_proj(x2d, w_pre, w_in_p, gup_p, gate_b, tm, act_dtype):
    n = x2d.shape[0]
    widths = (QK_W, QK_W, V_W, V_W, QK_W, SQ_W, SKV_W, SKV_W)
    dtypes = (act_dtype, act_dtype, act_dtype, act_dtype, F32, act_dtype, F32, F32)
    const = lambda a: pl.BlockSpec(a.shape, lambda i: (0, 0))
    return pl.pallas_call(
        _in_proj_kernel,
        grid=(n // tm,),
        in_specs=[pl.BlockSpec((tm, D_MODEL), lambda i: (i, 0)),
                  const(w_pre), const(w_in_p), const(gup_p), const(gate_b)],
        out_specs=[pl.BlockSpec((tm, w), lambda i: (i, 0)) for w in widths],
        out_shape=[jax.ShapeDtypeStruct((n, w), dt) for w, dt in zip(widths, dtypes)],
        compiler_params=pltpu.CompilerParams(
            dimension_semantics=("arbitrary",), vmem_limit_bytes=VMEM_LIMIT),
        name="in_proj",
    )(x2d, w_pre, w_in_p, gup_p, gate_b)


def _seg_cumsum(la, tri):
    a1 = la.astype(BF16)
    r1 = la - a1.astype(F32)
    a2 = r1.astype(BF16)
    a3 = (r1 - a2.astype(F32)).astype(BF16)
    g = _dot(tri, jnp.concatenate([a1, a2, a3], axis=1))
    return g[:, :QK_W] + g[:, QK_W:2 * QK_W] + g[:, 2 * QK_W:]


def _head_norm_gate(o, og, nw, e64):
    o2 = (o * o).astype(BF16)
    half = V_W // 2
    ms = jnp.concatenate([_dot(o2[:, :half], e64[:half, :half]),
                          _dot(o2[:, half:], e64[half:, half:])], axis=1)
    return o * lax.rsqrt(ms + EPS) * nw * (og * _sigmoid(og))


def _state_mask():
    r = lax.broadcasted_iota(jnp.int32, (QK_W, V_W), 0) >> 5
    c = lax.broadcasted_iota(jnp.int32, (QK_W, V_W), 1) >> 6
    return r == c


def _gla_chunk(q, k, v, la, tri, s, smask, intra):
    C = q.shape[0]
    g = _seg_cumsum(la, tri)
    k_t = k.T
    g_t = g.T
    g_last = g_t[:, C - 1:C]
    o = _dot((q * jnp.exp(g)).astype(BF16), s.astype(BF16)) + intra(q, g, k_t, g_t)
    upd = _dot((k_t * jnp.exp(g_last - g_t)).astype(BF16), v)
    return o, s * jnp.exp(g_last) + jnp.where(smask, upd, 0.0)


def _causal_mask():
    shape = (GLA_CHUNK, GLA_HEADS * GLA_CHUNK)
    row = lax.broadcasted_iota(jnp.int32, shape, 0)
    col = lax.broadcasted_iota(jnp.int32, shape, 1) & (GLA_CHUNK - 1)
    return col <= row


def _intra_factorised(v, causal):
    C = GLA_CHUNK
    hh = GLA_HEADS // 2

    def intra(q, g, k_t, g_t):
        qe = (q * jnp.exp(g - g[GLA_MID:GLA_MID + 1, :])).astype(BF16)
        k_inv = (k_t * jnp.exp(g_t[:, GLA_MID:GLA_MID + 1] - g_t)).astype(BF16)
        cols = []
        for h in range(GLA_HEADS):
            parts = [k_inv[GLA_DK * h:GLA_DK * (h + 1), :]]
            if h > 0:
                parts.insert(0, jnp.zeros((GLA_DK * h, C), BF16))
            if h < GLA_HEADS - 1:
                parts.append(jnp.zeros((QK_W - GLA_DK * (h + 1), C), BF16))
            cols.append(jnp.concatenate(parts, axis=0))
        a = jnp.where(causal, _dot(qe, jnp.concatenate(cols, axis=1)), 0.0).astype(BF16)
        low = lax.broadcasted_iota(jnp.int32, (C, 2 * GLA_DV), 1) < GLA_DV
        zt = jnp.zeros((C, 2 * GLA_DV), BF16)
        halves = []
        for n in range(2):
            blocks = []
            for h in range(hh * n, hh * (n + 1)):
                vt = v[:, 2 * GLA_DV * (h // 2):2 * GLA_DV * (h // 2 + 1)]
                sel = jnp.where(low, vt, zt) if h % 2 == 0 else jnp.where(low, zt, vt)
                blocks.append(jnp.concatenate([sel, zt] if (h // 2) % 2 == 0 else [zt, sel], axis=1))
            halves.append(_dot(a[:, hh * C * n:hh * C * (n + 1)], jnp.concatenate(blocks, axis=0)))
        return jnp.concatenate(halves, axis=1)

    return intra


def _gla_prompt_kernel(q_ref, k_ref, v_ref, la_ref, og_ref, nw_ref, tri_ref, e_ref, e64_ref,
                       o_ref, st_ref, sbd, kbuf, vbuf, gbuf, *, nblk, tb):
    C = GLA_CHUNK
    c = pl.program_id(1)

    @pl.when(c == 0)
    def _():
        sbd[...] = jnp.zeros_like(sbd)

    safe = jnp.min(la_ref[...]) >= GLA_SAFE_LOG_DECAY

    @pl.when(safe)
    def _():
        s = sbd[...]
        smask = _state_mask()
        causal = _causal_mask()
        outs = []
        for ci in range(tb // C):
            rows = slice(ci * C, (ci + 1) * C)
            v = v_ref[rows, :]
            o, s = _gla_chunk(q_ref[rows, :].astype(F32), k_ref[rows, :].astype(F32), v,
                              la_ref[rows, :], tri_ref[...], s, smask,
                              _intra_factorised(v, causal))
            outs.append(o)
        sbd[...] = s
        o_ref[...] = _head_norm_gate(jnp.concatenate(outs, axis=0), og_ref[...].astype(F32),
                                     nw_ref[...], e64_ref[...]).astype(o_ref.dtype)

    @pl.when(jnp.logical_not(safe))
    def _():
        def chunk(ci, carry):
            r = pl.multiple_of(ci * C, C)
            rows = pl.ds(r, C)
            kbuf[...] = k_ref[rows, :].astype(F32)
            vbuf[...] = v_ref[rows, :].astype(F32)

            def intra(q, g, k_t, g_t):
                gbuf[...] = g
                row = lax.broadcasted_iota(jnp.int32, (C, QK_W), 0)

                def key(j, acc):
                    m = row >= j
                    dec = jnp.where(m, jnp.exp(jnp.where(m, g - gbuf[pl.ds(j, 1), :], 0.0)), 0.0)
                    p = (q * kbuf[pl.ds(j, 1), :] * dec).astype(BF16)
                    return acc + _dot(p, e_ref[...]) * vbuf[pl.ds(j, 1), :]

                return lax.fori_loop(0, C, key, jnp.zeros((C, V_W), F32))

            o, s = _gla_chunk(q_ref[rows, :].astype(F32), kbuf[...], v_ref[rows, :],
                              la_ref[rows, :], tri_ref[...], sbd[...], _state_mask(), intra)
            sbd[...] = s
            o_ref[rows, :] = _head_norm_gate(o, og_ref[rows, :].astype(F32), nw_ref[...],
                                             e64_ref[...]).astype(o_ref.dtype)
            return carry

        lax.fori_loop(0, tb // C, chunk, 0)

    @pl.when(c == nblk - 1)
    def _():
        st_ref[0] = sbd[...]


def _gla_prompt(qg, kg, vg, la, og, nw, tri, e, e64, batch, seq, tb):
    nblk = seq // tb
    row = lambda w: pl.BlockSpec((tb, w), lambda b, c: (b * nblk + c, 0))
    const = lambda a: pl.BlockSpec(a.shape, lambda b, c: (0, 0))
    return pl.pallas_call(
        functools.partial(_gla_prompt_kernel, nblk=nblk, tb=tb),
        grid=(batch, nblk),
        in_specs=[row(QK_W), row(QK_W), row(V_W), row(QK_W), row(V_W),
                  const(nw), const(tri), const(e), const(e64)],
        out_specs=[row(V_W), pl.BlockSpec((1, QK_W, V_W), lambda b, c: (b, 0, 0))],
        out_shape=[jax.ShapeDtypeStruct((batch * seq, V_W), BF16),
                   jax.ShapeDtypeStruct((batch, QK_W, V_W), F32)],
        scratch_shapes=[pltpu.VMEM((QK_W, V_W), F32), pltpu.VMEM((GLA_CHUNK, QK_W), F32),
                        pltpu.VMEM((GLA_CHUNK, V_W), F32), pltpu.VMEM((GLA_CHUNK, QK_W), F32)],
        compiler_params=pltpu.CompilerParams(
            dimension_semantics=("arbitrary", "arbitrary"), vmem_limit_bytes=VMEM_LIMIT),
        name="gla_prompt",
    )(qg, kg, vg, la, og, nw, tri, e, e64)


GLA_SAMPLE_SEQS = 8


def _state_mask_t():
    r = lax.broadcasted_iota(jnp.int32, (V_W, QK_W), 0) >> 6
    c = lax.broadcasted_iota(jnp.int32, (V_W, QK_W), 1) >> 5
    return r == c


def _fold_state_t(y):
    head = lax.broadcasted_iota(jnp.int32, (GLA_DV, QK_W), 1) >> 5
    acc = jnp.zeros((GLA_DV, QK_W), F32)
    for h in range(GLA_HEADS):
        acc = acc + jnp.where(head == h, y[GLA_DV * h:GLA_DV * (h + 1), :], 0.0)
    return acc


def _gla_sample_kernel(q_ref, k_ref, v_ref, la_ref, og_ref, nw_ref, tri_ref, e_ref, e64_ref,
                       s0_ref, o_ref, s1_ref, *, t):
    ns = GLA_SAMPLE_SEQS
    rows = ns * t
    q = q_ref[...]
    k = k_ref[...]
    v = v_ref[...]
    g = _seg_cumsum(la_ref[...], tri_ref[...])
    g3 = g.reshape(ns, t, QK_W)
    q3 = q.reshape(ns, t, QK_W)
    k3 = k.reshape(ns, t, QK_W)
    v3 = v.reshape(ns, t, V_W)
    irow = lax.broadcasted_iota(jnp.int32, (ns, t, QK_W), 1)
    acc = jnp.zeros((rows, V_W), F32)
    for j in range(t):
        m = irow >= j
        dec = jnp.where(m, jnp.exp(jnp.where(m, g3 - g3[:, j:j + 1, :], 0.0)), 0.0)
        p = (q3 * k3[:, j:j + 1, :] * dec).reshape(rows, QK_W).astype(BF16)
        vj = jnp.broadcast_to(v3[:, j:j + 1, :], (ns, t, V_W)).reshape(rows, V_W)
        acc = acc + _dot(p, e_ref[...]) * vj

    qe = (q * jnp.exp(g)).astype(BF16)
    g_last = jnp.broadcast_to(g3[:, t - 1:t, :], (ns, t, QK_W)).reshape(rows, QK_W)
    kt = k * jnp.exp(g_last - g)
    v_bf = v.astype(BF16)
    seq_o = lax.broadcasted_iota(jnp.int32, (rows, V_W), 0) >> 3
    seq_k = lax.broadcasted_iota(jnp.int32, (rows, QK_W), 0) >> 3
    bdmask = _state_mask_t()
    for s in range(ns):
        s0 = s0_ref[s]
        sbd = jnp.where(bdmask, jnp.concatenate([s0] * GLA_HEADS, axis=0), 0.0).astype(BF16)
        acc = acc + jnp.where(seq_o == s, _dot_nt(qe, sbd), 0.0)
        kts = jnp.where(seq_k == s, kt, 0.0).astype(BF16)
        s1_ref[s] = (s0 * jnp.exp(g[t * s + t - 1:t * s + t, :])
                     + _fold_state_t(_dot_tn(v_bf, kts)))

    o_ref[...] = _head_norm_gate(acc, og_ref[...], nw_ref[...], e64_ref[...])


def _gla_sample(qg, kg, vg, la, og, nw, tri, e, e64, s0t, t):
    n = qg.shape[0]
    ns = GLA_SAMPLE_SEQS
    rows = ns * t
    row = lambda w: pl.BlockSpec((rows, w), lambda i: (i, 0))
    const = lambda a: pl.BlockSpec(a.shape, lambda i: (0, 0))
    st = pl.BlockSpec((ns, GLA_DV, QK_W), lambda i: (i, 0, 0))
    return pl.pallas_call(
        functools.partial(_gla_sample_kernel, t=t),
        grid=(n // rows,),
        in_specs=[row(QK_W), row(QK_W), row(V_W), row(QK_W), row(V_W),
                  const(nw), const(tri), const(e), const(e64), st],
        out_specs=[row(V_W), st],
        out_shape=[jax.ShapeDtypeStruct((n, V_W), F32),
                   jax.ShapeDtypeStruct(s0t.shape, F32)],
        compiler_params=pltpu.CompilerParams(
            dimension_semantics=("arbitrary",), vmem_limit_bytes=VMEM_LIMIT),
        name="gla_sample",
    )(qg, kg, vg, la, og, nw, tri, e, e64, s0t)


def _swa_attend(q, kband, vband, bias, sink):
    r = q.shape[0]
    low = lax.broadcasted_iota(jnp.int32, (r, SKV_W), 1) < SWA_DIM
    zero = jnp.zeros((r, SKV_W), q.dtype)
    pieces = []
    for p in range(4):
        blk = q[:, SKV_W * p:SKV_W * (p + 1)]
        pieces.append(jnp.where(low, blk, zero))
        pieces.append(jnp.where(low, zero, blk))
    lhs = jnp.concatenate(pieces, axis=0).astype(BF16)
    s = _dot_nt(lhs, kband) + bias
    m = jnp.maximum(jnp.max(s, axis=-1, keepdims=True), sink)
    e = jnp.exp(s - jnp.concatenate([m, m], axis=1)).astype(BF16)
    pv = _dot(e, vband)
    o = pv[:, :SKV_W] / (pv[:, SKV_W:] + jnp.exp(sink - m))
    outs = [jnp.where(low, o[2 * p * r:(2 * p + 1) * r], o[(2 * p + 1) * r:(2 * p + 2) * r])
            for p in range(4)]
    return jnp.concatenate(outs, axis=1)


SWA_Q_BLOCKS = 4


def _swa_prompt_kernel(q_ref, kp_ref, kc_ref, vp_ref, vc_ref, bias_ref, sink_ref, o_ref):
    L = WINDOW
    kall = jnp.concatenate([kp_ref[...], kc_ref[...]], axis=0).astype(BF16)
    vall = jnp.concatenate([vp_ref[...], vc_ref[...]], axis=0).astype(BF16)
    vall = jnp.concatenate([vall, jnp.ones_like(vall)], axis=1)
    bias = bias_ref[...]
    sink = sink_ref[...]
    key = lax.broadcasted_iota(jnp.int32, bias.shape, 1)
    bias0 = jnp.where(jnp.logical_or(pl.program_id(1) > 0, key >= L), bias, -jnp.inf)
    outs = []
    for sb in range(SWA_Q_BLOCKS):
        band = slice(L * sb, L * (sb + 2))
        outs.append(_swa_attend(q_ref[L * sb:L * (sb + 1), :], kall[band], vall[band],
                                bias0 if sb == 0 else bias, sink))
    o_ref[...] = jnp.concatenate(outs, axis=0).astype(o_ref.dtype)


def _swa_prompt(qs, ks, vs, bias, sink, batch, seq):
    L = WINDOW
    qb = SWA_Q_BLOCKS * L
    nb = seq // qb
    cur = lambda w: pl.BlockSpec((qb, w), lambda b, i: (b * nb + i, 0))
    prev = lambda w: pl.BlockSpec(
        (L, w), lambda b, i: ((b * nb + i) * SWA_Q_BLOCKS - jnp.minimum(i, 1), 0))
    const = lambda a: pl.BlockSpec(a.shape, lambda b, i: (0, 0))
    return pl.pallas_call(
        _swa_prompt_kernel,
        grid=(batch, nb),
        in_specs=[cur(SQ_W), prev(SKV_W), cur(SKV_W), prev(SKV_W), cur(SKV_W),
                  const(bias), const(sink)],
        out_specs=cur(SQ_W),
        out_shape=jax.ShapeDtypeStruct((batch * seq, SQ_W), BF16),
        compiler_params=pltpu.CompilerParams(
            dimension_semantics=("arbitrary", "arbitrary"), vmem_limit_bytes=VMEM_LIMIT),
        name="swa_prompt",
    )(qs, ks, ks, vs, vs, bias, sink)


SWA_SAMPLE_SEQS = 16


def _swa_sample_kernel(q_ref, kn_ref, vn_ref, kc_ref, vc_ref, bias_ref, sink_ref,
                       o_ref, ko_ref, vo_ref, *, t):
    wb = kc_ref.shape[1]
    pad = jnp.zeros((WINDOW - t, SKV_W), F32)

    def seq(s, carry):
        kn = kn_ref[s]
        vn = vn_ref[s]
        kc = kc_ref[s]
        vc = vc_ref[s]
        kband = jnp.concatenate([kc, kn, pad], axis=0).astype(BF16)
        vband = jnp.concatenate([vc, vn, pad], axis=0).astype(BF16)
        vband = jnp.concatenate([vband, jnp.ones_like(vband)], axis=1)
        o_ref[s] = _swa_attend(q_ref[s], kband, vband, bias_ref[...], sink_ref[...])
        ko_ref[s] = jnp.concatenate([kc[t:wb], kn], axis=0)
        vo_ref[s] = jnp.concatenate([vc[t:wb], vn], axis=0)
        return carry

    lax.fori_loop(0, SWA_SAMPLE_SEQS, seq, 0)


def _swa_sample(qs3, kn3, vn3, kc3, vc3, bias, sink, t):
    nseq = qs3.shape[0]
    ns = SWA_SAMPLE_SEQS
    blk = lambda a: pl.BlockSpec((ns,) + a.shape[1:], lambda i: (i, 0, 0))
    const = lambda a: pl.BlockSpec(a.shape, lambda i: (0, 0))
    return pl.pallas_call(
        functools.partial(_swa_sample_kernel, t=t),
        grid=(nseq // ns,),
        in_specs=[blk(qs3), blk(kn3), blk(vn3), blk(kc3), blk(vc3), const(bias), const(sink)],
        out_specs=[blk(qs3), blk(kc3), blk(vc3)],
        out_shape=[jax.ShapeDtypeStruct(qs3.shape, F32),
                   jax.ShapeDtypeStruct(kc3.shape, F32),
                   jax.ShapeDtypeStruct(vc3.shape, F32)],
        compiler_params=pltpu.CompilerParams(
            dimension_semantics=("arbitrary",), vmem_limit_bytes=VMEM_LIMIT),
        name="swa_sample",
    )(qs3, kn3, vn3, kc3, vc3, bias, sink)


def _swa_bias_table(r, sample):
    rows = np.arange(8 * r)
    piece = rows // r
    head = piece // 2 + 4 * (piece % 2)
    i = (rows % r)[:, None]
    slope = (2.0 ** -(head + 1.0))[:, None]
    key = np.arange(2 * WINDOW)[None, :]
    if sample:
        dist = np.where(key < WINDOW, WINDOW + i - key, i - (key - WINDOW))
        valid = (dist >= 0) & (dist < WINDOW) & (key < WINDOW + r)
    else:
        dist = i + WINDOW - key
        valid = (dist >= 0) & (dist < WINDOW)
    bias = np.where(valid, -slope * dist, -np.inf).astype(np.float32)
    return bias, head


def _post_kernel(og_ref, os_ref, x_ref, p_ref, wog_ref, wos_ref, wpm_ref, wpf_ref, wg_ref,
                 wu_ref, wd_ref, wpo_ref, wpg_ref, wpp_ref, y_ref):
    mix = _dot(og_ref[...].astype(BF16), wog_ref[...]) + _dot(os_ref[...].astype(BF16), wos_ref[...])
    h = x_ref[...] + _rms(mix, wpm_ref[...])
    f = _rms(h, wpf_ref[...]).astype(BF16)
    gate = _dot(f, wg_ref[...])
    up = _dot(f, wu_ref[...])
    act = (gate * _sigmoid(gate) * up).astype(BF16)
    h = h + _rms(_dot(act, wd_ref[...]), wpo_ref[...])
    ple = _sigmoid(_dot(h.astype(BF16), wpg_ref[...])) * _dot(p_ref[...].astype(BF16), wpp_ref[...])
    y_ref[...] = h + ple


def _post(og, osw, x2d, p2d, weights, tm):
    n = x2d.shape[0]
    row = lambda w: pl.BlockSpec((tm, w), lambda i: (i, 0))
    const = lambda a: pl.BlockSpec(a.shape, lambda i: (0, 0), pipeline_mode=pl.Buffered(1))
    return pl.pallas_call(
        _post_kernel,
        grid=(n // tm,),
        in_specs=[row(V_W), row(SQ_W), row(D_MODEL), row(PLE_DIM)] + [const(w) for w in weights],
        out_specs=row(D_MODEL),
        out_shape=jax.ShapeDtypeStruct((n, D_MODEL), F32),
        compiler_params=pltpu.CompilerParams(
            dimension_semantics=("arbitrary",), vmem_limit_bytes=VMEM_LIMIT),
        name="post",
    )(og, osw, x2d, p2d, *weights)


def _gla_constants(t_rows, seg):
    i = np.arange(t_rows)
    tri = ((i[:, None] >= i[None, :]) & (i[:, None] // seg == i[None, :] // seg))
    e = (np.arange(QK_W)[:, None] // GLA_DK == np.arange(V_W)[None, :] // GLA_DV)
    e64 = (np.arange(V_W)[:, None] // GLA_DV == np.arange(V_W)[None, :] // GLA_DV) / GLA_DV
    return (jnp.asarray(tri, BF16), jnp.asarray(e, BF16), jnp.asarray(e64, BF16))


def _state_from_blocks(st):
    b = st.shape[0]
    st = st.reshape(b, GLA_HEADS, GLA_DK, GLA_HEADS, GLA_DV)
    return jnp.stack([st[:, h, :, h, :] for h in range(GLA_HEADS)], axis=1)


def _state_from_t(st):
    b = st.shape[0]
    return st.reshape(b, GLA_DV, GLA_HEADS, GLA_DK).transpose(0, 2, 3, 1)


def _state_to_t(s):
    b = s.shape[0]
    return s.transpose(0, 3, 1, 2).reshape(b, GLA_DV, QK_W)


def kernel(x_prompt, x_sample, p_prompt, p_sample, state_gla, cache_swa_k, cache_swa_v,
           w_pre_mix, w_in, gla_gate_up, gla_gate_b, gla_norm_w, swa_sinks, w_out,
           w_post_mix, w_pre_ffn, w_gate, w_up, w_down, w_post_ffn, w_ple_gate, w_ple_proj):
    assert w_in.shape[0] == 1, "one trunk layer"
    batch, seq, _ = x_prompt.shape
    dec_batch, dec_seq, _ = x_sample.shape
    assert seq % 512 == 0 and (dec_batch * dec_seq) % 512 == 0
    assert dec_seq == 8, "a sample sequence is one 8-row sublane group"
    assert dec_batch % SWA_SAMPLE_SEQS == 0 and dec_batch % GLA_SAMPLE_SEQS == 0
    assert cache_swa_k.shape[2] == WINDOW
    order = np.asarray(SWA_HEAD_ORDER)

    parts = jnp.split(w_in[0], np.cumsum(IN_SPLITS)[:-1].tolist(), axis=1)
    w_qg, w_kg, w_vg, w_og, w_lr, w_qs, w_ks, w_vs = parts
    w_qs = w_qs.reshape(D_MODEL, SWA_HEADS, SWA_DIM)[:, order, :].reshape(D_MODEL, SQ_W)
    w_lr = jnp.pad(w_lr, ((0, 0), (0, LR_PAD - GLA_LOW_RANK)))
    w_in_p = jnp.concatenate([w_qg, w_kg, w_vg, w_og, w_qs, w_ks, w_vs, w_lr], axis=1).astype(BF16)
    gup_p = jnp.pad(gla_gate_up[0], ((0, LR_PAD - GLA_LOW_RANK), (0, 0))).astype(BF16)
    gate_b = gla_gate_b[0].reshape(1, QK_W)
    w_pre = w_pre_mix[0].reshape(1, D_MODEL)
    norm_w = jnp.tile(gla_norm_w[0], GLA_HEADS).reshape(1, V_W)
    w_out_g = w_out[0, :V_W].astype(BF16)
    w_out_s = (w_out[0, V_W:].reshape(SWA_HEADS, SWA_DIM, D_MODEL)[order]
               .reshape(SQ_W, D_MODEL).astype(BF16))
    post_w = (w_out_g, w_out_s, w_post_mix[0].reshape(1, D_MODEL), w_pre_ffn[0].reshape(1, D_MODEL),
              w_gate[0].astype(BF16), w_up[0].astype(BF16), w_down[0].astype(BF16),
              w_post_ffn[0].reshape(1, D_MODEL), w_ple_gate[0].astype(BF16),
              w_ple_proj[0].astype(BF16))

    xp = x_prompt.reshape(batch * seq, D_MODEL)
    qg, kg, vg, og, la, qs, ks, vs = _in_proj(xp, w_pre, w_in_p, gup_p, gate_b, 512, BF16)
    tri, e, e64 = _gla_constants(GLA_CHUNK, GLA_CHUNK)
    o_g, st_p = _gla_prompt(qg, kg, vg, la, og, norm_w, tri, e, e64, batch, seq, tb=512)
    bias_np, head_np = _swa_bias_table(WINDOW, sample=False)
    sink_p = jnp.broadcast_to(swa_sinks[0][head_np][:, None], (head_np.size, SKV_W))
    o_s = _swa_prompt(qs, ks, vs, jnp.asarray(bias_np), sink_p, batch, seq)
    y_p = _post(o_g, o_s, xp, p_prompt[0].reshape(batch * seq, PLE_DIM), post_w, tm=256)
    keep = min(WINDOW, seq)
    k_p = ks.reshape(batch, seq, SWA_KV_HEADS, SWA_DIM)[:, seq - keep:]
    v_p = vs.reshape(batch, seq, SWA_KV_HEADS, SWA_DIM)[:, seq - keep:]

    n_s = dec_batch * dec_seq
    xs = x_sample.reshape(n_s, D_MODEL)
    qg, kg, vg, og, la, qs, ks, vs = _in_proj(xs, w_pre, w_in_p, gup_p, gate_b, 512, F32)
    tri_s, _, _ = _gla_constants(GLA_SAMPLE_SEQS * dec_seq, dec_seq)
    o_g, st_s = _gla_sample(qg, kg, vg, la, og, norm_w, tri_s, e, e64,
                            _state_to_t(state_gla[0]), dec_seq)
    bias_np, head_np = _swa_bias_table(dec_seq, sample=True)
    sink_s = jnp.broadcast_to(swa_sinks[0][head_np][:, None], (head_np.size, SKV_W))
    wb = cache_swa_k.shape[2]
    o_s3, k_s, v_s = _swa_sample(
        qs.reshape(dec_batch, dec_seq, SQ_W), ks.reshape(dec_batch, dec_seq, SKV_W),
        vs.reshape(dec_batch, dec_seq, SKV_W), cache_swa_k[0].reshape(dec_batch, wb, SKV_W),
        cache_swa_v[0].reshape(dec_batch, wb, SKV_W), jnp.asarray(bias_np), sink_s, dec_seq)
    y_s = _post(o_g, o_s3.reshape(n_s, SQ_W), xs, p_sample[0].reshape(n_s, PLE_DIM), post_w, tm=256)

    return (y_p.reshape(batch, seq, D_MODEL),
            y_s.reshape(dec_batch, dec_seq, D_MODEL),
            _state_from_blocks(st_p)[None],
            k_p[None], v_p[None],
            _state_from_t(st_s).astype(state_gla.dtype)[None],
            k_s.reshape(dec_batch, wb, SWA_KV_HEADS, SWA_DIM)[None],
            v_s.reshape(dec_batch, wb, SWA_KV_HEADS, SWA_DIM)[None])
```

```python
import functools

import numpy as np
import jax
import jax.numpy as jnp
from jax import lax
from jax.experimental import pallas as pl
from jax.experimental.pallas import tpu as pltpu

F32 = jnp.float32
BF16 = jnp.bfloat16

D_MODEL = 1024
GLA_HEADS = 8
GLA_DK = 32
GLA_DV = 64
GLA_LOW_RANK = 16
GLA_GATE_NORM = 16.0
SWA_HEADS = 8
SWA_KV_HEADS = 2
SWA_DIM = 64
WINDOW = 128
D_FF = 2816
PLE_DIM = 256
EPS = 1e-6

QK_W = GLA_HEADS * GLA_DK
V_W = GLA_HEADS * GLA_DV
SQ_W = SWA_HEADS * SWA_DIM
SKV_W = SWA_KV_HEADS * SWA_DIM
LR_PAD = 128
IN_SPLITS = (QK_W, QK_W, V_W, V_W, GLA_LOW_RANK, SQ_W, SKV_W, SKV_W)

C_QG, C_KG, C_VG, C_OG, C_QS, C_KS, C_VS, C_LR = 0, 256, 512, 1024, 1536, 2048, 2176, 2304
D_IN_P = C_KS + 512

SWA_HEAD_ORDER = (0, 4, 1, 5, 2, 6, 3, 7)

GLA_CHUNK = 128
GLA_MID = GLA_CHUNK // 2 - 1
GLA_SAFE_LOG_DECAY = -60.0 / (GLA_CHUNK // 2)

VMEM_LIMIT = 56 * 1024 * 1024


def _sigmoid(x):
    return 1.0 / (1.0 + jnp.exp(-x))


def _rms(x, w):
    ms = jnp.mean(x * x, axis=-1, keepdims=True)
    return x * lax.rsqrt(ms + EPS) * w


def _dot(a, b):
    return jnp.dot(a, b, preferred_element_type=F32)


def _dot_nt(a, b):
    return lax.dot_general(a, b, (((1,), (1,)), ((), ())), preferred_element_type=F32)


def _dot_tn(a, b):
    return lax.dot_general(a, b, (((0,), (0,)), ((), ())), preferred_element_type=F32)


IN_ROW_GROUPS = 2
IN_TILE = 512


def _in_proj_kernel(x_ref, wpre_ref, win_ref, gup_ref, gb_ref,
                    qg_ref, kg_ref, vg_ref, og_ref, la_ref, qs_ref, ks_ref, vs_ref, lamin_ref):
    def rows_chain(rows):
        hn = _rms(x_ref[rows, :], wpre_ref[...]).astype(BF16)

        def proj(lo, hi):
            return _dot(hn, win_ref[:, lo:hi])

        tail = proj(C_KS, D_IN_P)
        lr = tail[:, C_LR - C_KS:C_LR - C_KS + LR_PAD].astype(BF16)
        pre = _dot(lr, gup_ref[...]) + gb_ref[...]
        log_sig = jnp.minimum(pre, 0.0) - jnp.log1p(jnp.exp(-jnp.abs(pre)))
        return (proj(C_QG, C_KG) * (GLA_DK ** -0.5), proj(C_KG, C_VG), proj(C_VG, C_OG),
                proj(C_OG, C_QS), log_sig * (1.0 / GLA_GATE_NORM),
                proj(C_QS, C_KS) * (SWA_DIM ** -0.5), tail[:, :SKV_W], tail[:, SKV_W:2 * SKV_W])

    tm = x_ref.shape[0]
    sub = tm // IN_ROW_GROUPS
    groups = [rows_chain(slice(i * sub, (i + 1) * sub)) for i in range(IN_ROW_GROUPS)]
    refs = (qg_ref, kg_ref, vg_ref, og_ref, la_ref, qs_ref, ks_ref, vs_ref)
    for j, ref in enumerate(refs):
        ref[...] = jnp.concatenate([g[j] for g in groups], axis=0).astype(ref.dtype)
    la_min = jnp.min(jnp.concatenate([g[4] for g in groups], axis=0), keepdims=True)
    lamin_ref[...] = jnp.broadcast_to(la_min.reshape(1, 1, 1), lamin_ref.shape)


def _in_proj(x2d, w_pre, w_in_p, gup_p, gate_b, act_dtype):
    n = x2d.shape[0]
    tm = IN_TILE
    widths = (QK_W, QK_W, V_W, V_W, QK_W, SQ_W, SKV_W, SKV_W)
    dtypes = (act_dtype, act_dtype, act_dtype, act_dtype, F32, act_dtype, F32, F32)
    const = lambda a: pl.BlockSpec(a.shape, lambda i: (0, 0))
    return pl.pallas_call(
        _in_proj_kernel,
        grid=(n // tm,),
        in_specs=[pl.BlockSpec((tm, D_MODEL), lambda i: (i, 0)),
                  const(w_pre), const(w_in_p), const(gup_p), const(gate_b)],
        out_specs=[pl.BlockSpec((tm, w), lambda i: (i, 0)) for w in widths]
        + [pl.BlockSpec((1, 8, 128), lambda i: (i, 0, 0))],
        out_shape=[jax.ShapeDtypeStruct((n, w), dt) for w, dt in zip(widths, dtypes)]
        + [jax.ShapeDtypeStruct((n // tm, 8, 128), F32)],
        compiler_params=pltpu.CompilerParams(
            dimension_semantics=("arbitrary",), vmem_limit_bytes=VMEM_LIMIT),
        name="in_proj",
    )(x2d, w_pre, w_in_p, gup_p, gate_b)


def _seg_cumsum(la, tri):
    a1 = la.astype(BF16)
    r1 = la - a1.astype(F32)
    a2 = r1.astype(BF16)
    a3 = (r1 - a2.astype(F32)).astype(BF16)
    g = _dot(tri, jnp.concatenate([a1, a2, a3], axis=1))
    return g[:, :QK_W] + g[:, QK_W:2 * QK_W] + g[:, 2 * QK_W:]


def _head_norm_gate(o, og, nw, e64):
    o2 = (o * o).astype(BF16)
    half = V_W // 2
    ms = jnp.concatenate([_dot(o2[:, :half], e64[:half, :half]),
                          _dot(o2[:, half:], e64[half:, half:])], axis=1)
    return o * lax.rsqrt(ms + EPS) * nw * (og * _sigmoid(og))


def _state_mask():
    r = lax.broadcasted_iota(jnp.int32, (QK_W, V_W), 0) >> 5
    c = lax.broadcasted_iota(jnp.int32, (QK_W, V_W), 1) >> 6
    return r == c


def _gla_chunk(q, k, v, la, tri, s, smask, intra):
    C = q.shape[0]
    g = _seg_cumsum(la, tri)
    k_t = k.T
    g_t = g.T
    g_last = g_t[:, C - 1:C]
    o = _dot((q * jnp.exp(g)).astype(BF16), s.astype(BF16)) + intra(q, g, k_t, g_t)
    upd = _dot((k_t * jnp.exp(g_last - g_t)).astype(BF16), v)
    return o, s * jnp.exp(g_last) + jnp.where(smask, upd, 0.0)


def _causal_mask():
    shape = (GLA_CHUNK, GLA_HEADS * GLA_CHUNK)
    row = lax.broadcasted_iota(jnp.int32, shape, 0)
    col = lax.broadcasted_iota(jnp.int32, shape, 1) & (GLA_CHUNK - 1)
    return col <= row


def _intra_factorised(v, causal):
    C = GLA_CHUNK
    hh = GLA_HEADS // 2

    def intra(q, g, k_t, g_t):
        qe = (q * jnp.exp(g - g[GLA_MID:GLA_MID + 1, :])).astype(BF16)
        k_inv = (k_t * jnp.exp(g_t[:, GLA_MID:GLA_MID + 1] - g_t)).astype(BF16)
        cols = []
        for h in range(GLA_HEADS):
            parts = [k_inv[GLA_DK * h:GLA_DK * (h + 1), :]]
            if h > 0:
                parts.insert(0, jnp.zeros((GLA_DK * h, C), BF16))
            if h < GLA_HEADS - 1:
                parts.append(jnp.zeros((QK_W - GLA_DK * (h + 1), C), BF16))
            cols.append(jnp.concatenate(parts, axis=0))
        a = jnp.where(causal, _dot(qe, jnp.concatenate(cols, axis=1)), 0.0).astype(BF16)
        low = lax.broadcasted_iota(jnp.int32, (C, 2 * GLA_DV), 1) < GLA_DV
        zt = jnp.zeros((C, 2 * GLA_DV), BF16)
        halves = []
        for n in range(2):
            blocks = []
            for h in range(hh * n, hh * (n + 1)):
                vt = v[:, 2 * GLA_DV * (h // 2):2 * GLA_DV * (h // 2 + 1)]
                sel = jnp.where(low, vt, zt) if h % 2 == 0 else jnp.where(low, zt, vt)
                blocks.append(jnp.concatenate([sel, zt] if (h // 2) % 2 == 0 else [zt, sel], axis=1))
            halves.append(_dot(a[:, hh * C * n:hh * C * (n + 1)], jnp.concatenate(blocks, axis=0)))
        return jnp.concatenate(halves, axis=1)

    return intra


def _gla_prompt_kernel(safe_ref, q_ref, k_ref, v_ref, la_ref, og_ref, nw_ref, tri_ref, e_ref,
                       e64_ref, o_ref, st_ref, sbd, kbuf, vbuf, gbuf, *, nblk, tb):
    C = GLA_CHUNK
    c = pl.program_id(1)

    @pl.when(c == 0)
    def _():
        sbd[...] = jnp.zeros_like(sbd)

    safe = safe_ref[pl.program_id(0) * nblk + c] != 0

    @pl.when(safe)
    def _():
        s = sbd[...]
        smask = _state_mask()
        causal = _causal_mask()
        outs = []
        for ci in range(tb // C):
            rows = slice(ci * C, (ci + 1) * C)
            v = v_ref[rows, :]
            o, s = _gla_chunk(q_ref[rows, :].astype(F32), k_ref[rows, :].astype(F32), v,
                              la_ref[rows, :], tri_ref[...], s, smask,
                              _intra_factorised(v, causal))
            outs.append(o)
        sbd[...] = s
        o_ref[...] = _head_norm_gate(jnp.concatenate(outs, axis=0), og_ref[...].astype(F32),
                                     nw_ref[...], e64_ref[...]).astype(o_ref.dtype)

    @pl.when(jnp.logical_not(safe))
    def _():
        def chunk(ci, carry):
            r = pl.multiple_of(ci * C, C)
            rows = pl.ds(r, C)
            kbuf[...] = k_ref[rows, :].astype(F32)
            vbuf[...] = v_ref[rows, :].astype(F32)

            def intra(q, g, k_t, g_t):
                gbuf[...] = g
                row = lax.broadcasted_iota(jnp.int32, (C, QK_W), 0)

                def key(j, acc):
                    m = row >= j
                    dec = jnp.where(m, jnp.exp(jnp.where(m, g - gbuf[pl.ds(j, 1), :], 0.0)), 0.0)
                    p = (q * kbuf[pl.ds(j, 1), :] * dec).astype(BF16)
                    return acc + _dot(p, e_ref[...]) * vbuf[pl.ds(j, 1), :]

                return lax.fori_loop(0, C, key, jnp.zeros((C, V_W), F32))

            o, s = _gla_chunk(q_ref[rows, :].astype(F32), kbuf[...], v_ref[rows, :],
                              la_ref[rows, :], tri_ref[...], sbd[...], _state_mask(), intra)
            sbd[...] = s
            o_ref[rows, :] = _head_norm_gate(o, og_ref[rows, :].astype(F32), nw_ref[...],
                                             e64_ref[...]).astype(o_ref.dtype)
            return carry

        lax.fori_loop(0, tb // C, chunk, 0)

    @pl.when(c == nblk - 1)
    def _():
        st_ref[0] = sbd[...]


def _gla_prompt(safe, qg, kg, vg, la, og, nw, tri, e, e64, batch, seq, tb):
    nblk = seq // tb
    row = lambda w: pl.BlockSpec((tb, w), lambda b, c, s: (b * nblk + c, 0))
    const = lambda a: pl.BlockSpec(a.shape, lambda b, c, s: (0, 0))
    return pl.pallas_call(
        functools.partial(_gla_prompt_kernel, nblk=nblk, tb=tb),
        grid_spec=pltpu.PrefetchScalarGridSpec(
            num_scalar_prefetch=1,
            grid=(batch, nblk),
            in_specs=[row(QK_W), row(QK_W), row(V_W), row(QK_W), row(V_W),
                      const(nw), const(tri), const(e), const(e64)],
            out_specs=[row(V_W), pl.BlockSpec((1, QK_W, V_W), lambda b, c, s: (b, 0, 0))],
            scratch_shapes=[pltpu.VMEM((QK_W, V_W), F32), pltpu.VMEM((GLA_CHUNK, QK_W), F32),
                            pltpu.VMEM((GLA_CHUNK, V_W), F32), pltpu.VMEM((GLA_CHUNK, QK_W), F32)]),
        out_shape=[jax.ShapeDtypeStruct((batch * seq, V_W), BF16),
                   jax.ShapeDtypeStruct((batch, QK_W, V_W), F32)],
        compiler_params=pltpu.CompilerParams(
            dimension_semantics=("arbitrary", "arbitrary"), vmem_limit_bytes=VMEM_LIMIT),
        name="gla_prompt",
    )(safe, qg, kg, vg, la, og, nw, tri, e, e64)


GLA_SAMPLE_SEQS = 8


def _state_mask_t():
    r = lax.broadcasted_iota(jnp.int32, (V_W, QK_W), 0) >> 6
    c = lax.broadcasted_iota(jnp.int32, (V_W, QK_W), 1) >> 5
    return r == c


def _fold_state_t(y):
    head = lax.broadcasted_iota(jnp.int32, (GLA_DV, QK_W), 1) >> 5
    acc = jnp.zeros((GLA_DV, QK_W), F32)
    for h in range(GLA_HEADS):
        acc = acc + jnp.where(head == h, y[GLA_DV * h:GLA_DV * (h + 1), :], 0.0)
    return acc


def _gla_sample_kernel(q_ref, k_ref, v_ref, la_ref, og_ref, nw_ref, tri_ref, e_ref, e64_ref,
                       s0_ref, o_ref, s1_ref, *, t):
    ns = GLA_SAMPLE_SEQS
    rows = ns * t
    q = q_ref[...]
    k = k_ref[...]
    v = v_ref[...]
    g = _seg_cumsum(la_ref[...], tri_ref[...])
    g3 = g.reshape(ns, t, QK_W)
    q3 = q.reshape(ns, t, QK_W)
    k3 = k.reshape(ns, t, QK_W)
    v3 = v.reshape(ns, t, V_W)
    irow = lax.broadcasted_iota(jnp.int32, (ns, t, QK_W), 1)
    acc = jnp.zeros((rows, V_W), F32)
    for j in range(t):
        m = irow >= j
        dec = jnp.where(m, jnp.exp(jnp.where(m, g3 - g3[:, j:j + 1, :], 0.0)), 0.0)
        p = (q3 * k3[:, j:j + 1, :] * dec).reshape(rows, QK_W).astype(BF16)
        vj = jnp.broadcast_to(v3[:, j:j + 1, :], (ns, t, V_W)).reshape(rows, V_W)
        acc = acc + _dot(p, e_ref[...]) * vj

    qe = (q * jnp.exp(g)).astype(BF16)
    g_last = jnp.broadcast_to(g3[:, t - 1:t, :], (ns, t, QK_W)).reshape(rows, QK_W)
    kt = k * jnp.exp(g_last - g)
    v_bf = v.astype(BF16)
    seq_o = lax.broadcasted_iota(jnp.int32, (rows, V_W), 0) >> 3
    seq_k = lax.broadcasted_iota(jnp.int32, (rows, QK_W), 0) >> 3
    bdmask = _state_mask_t()
    for s in range(ns):
        s0 = s0_ref[s]
        sbd = jnp.where(bdmask, jnp.concatenate([s0] * GLA_HEADS, axis=0), 0.0).astype(BF16)
        acc = acc + jnp.where(seq_o == s, _dot_nt(qe, sbd), 0.0)
        kts = jnp.where(seq_k == s, kt, 0.0).astype(BF16)
        s1_ref[s] = (s0 * jnp.exp(g[t * s + t - 1:t * s + t, :])
                     + _fold_state_t(_dot_tn(v_bf, kts)))

    o_ref[...] = _head_norm_gate(acc, og_ref[...], nw_ref[...], e64_ref[...])


def _gla_sample(qg, kg, vg, la, og, nw, tri, e, e64, s0t, t):
    n = qg.shape[0]
    ns = GLA_SAMPLE_SEQS
    rows = ns * t
    row = lambda w: pl.BlockSpec((rows, w), lambda i: (i, 0))
    const = lambda a: pl.BlockSpec(a.shape, lambda i: (0, 0))
    st = pl.BlockSpec((ns, GLA_DV, QK_W), lambda i: (i, 0, 0))
    return pl.pallas_call(
        functools.partial(_gla_sample_kernel, t=t),
        grid=(n // rows,),
        in_specs=[row(QK_W), row(QK_W), row(V_W), row(QK_W), row(V_W),
                  const(nw), const(tri), const(e), const(e64), st],
        out_specs=[row(V_W), st],
        out_shape=[jax.ShapeDtypeStruct((n, V_W), F32),
                   jax.ShapeDtypeStruct(s0t.shape, F32)],
        compiler_params=pltpu.CompilerParams(
            dimension_semantics=("arbitrary",), vmem_limit_bytes=VMEM_LIMIT),
        name="gla_sample",
    )(qg, kg, vg, la, og, nw, tri, e, e64, s0t)


def _swa_attend(q, kband, vband, bias, sink_ref):
    r = q.shape[0]
    sink = jnp.concatenate([jnp.full((r, SKV_W), sink_ref[pc // 2 + 4 * (pc % 2)], F32)
                            for pc in range(SWA_HEADS)], axis=0)
    low = lax.broadcasted_iota(jnp.int32, (r, SKV_W), 1) < SWA_DIM
    zero = jnp.zeros((r, SKV_W), q.dtype)
    pieces = []
    for p in range(4):
        blk = q[:, SKV_W * p:SKV_W * (p + 1)]
        pieces.append(jnp.where(low, blk, zero))
        pieces.append(jnp.where(low, zero, blk))
    lhs = jnp.concatenate(pieces, axis=0).astype(BF16)
    s = _dot_nt(lhs, kband) + bias
    m = jnp.maximum(jnp.max(s, axis=-1, keepdims=True), sink)
    pv = _dot(jnp.exp(s - jnp.concatenate([m, m], axis=1)).astype(BF16), vband)
    o = pv[:, :SKV_W] / (pv[:, SKV_W:] + jnp.exp(sink - m))
    outs = [jnp.where(low, o[2 * p * r:(2 * p + 1) * r], o[(2 * p + 1) * r:(2 * p + 2) * r])
            for p in range(4)]
    return jnp.concatenate(outs, axis=1)


SWA_Q_BLOCKS = 4


def _swa_prompt_kernel(q_ref, kp_ref, kc_ref, vp_ref, vc_ref, bias_ref, sink_ref, o_ref):
    L = WINDOW
    kall = jnp.concatenate([kp_ref[...], kc_ref[...]], axis=0).astype(BF16)
    vall = jnp.concatenate([vp_ref[...], vc_ref[...]], axis=0).astype(BF16)
    vall = jnp.concatenate([vall, jnp.ones_like(vall)], axis=1)
    bias = bias_ref[...]
    key = lax.broadcasted_iota(jnp.int32, bias.shape, 1)
    bias0 = jnp.where(jnp.logical_or(pl.program_id(1) > 0, key >= L), bias, -jnp.inf)
    outs = []
    for sb in range(SWA_Q_BLOCKS):
        band = slice(L * sb, L * (sb + 2))
        outs.append(_swa_attend(q_ref[L * sb:L * (sb + 1), :], kall[band], vall[band],
                                bias0 if sb == 0 else bias, sink_ref))
    o_ref[...] = jnp.concatenate(outs, axis=0).astype(o_ref.dtype)


def _swa_prompt(qs, ks, vs, bias, sink, batch, seq):
    L = WINDOW
    qb = SWA_Q_BLOCKS * L
    nb = seq // qb
    cur = lambda w: pl.BlockSpec((qb, w), lambda b, i: (b * nb + i, 0))
    prev = lambda w: pl.BlockSpec(
        (L, w), lambda b, i: ((b * nb + i) * SWA_Q_BLOCKS - jnp.minimum(i, 1), 0))
    const = lambda a: pl.BlockSpec(a.shape, lambda b, i: (0, 0))
    return pl.pallas_call(
        _swa_prompt_kernel,
        grid=(batch, nb),
        in_specs=[cur(SQ_W), prev(SKV_W), cur(SKV_W), prev(SKV_W), cur(SKV_W),
                  const(bias), pl.BlockSpec(memory_space=pltpu.SMEM)],
        out_specs=cur(SQ_W),
        out_shape=jax.ShapeDtypeStruct((batch * seq, SQ_W), BF16),
        compiler_params=pltpu.CompilerParams(
            dimension_semantics=("arbitrary", "arbitrary"), vmem_limit_bytes=VMEM_LIMIT),
        name="swa_prompt",
    )(qs, ks, ks, vs, vs, bias, sink)


SWA_SAMPLE_SEQS = 16


def _swa_sample_kernel(q_ref, kn_ref, vn_ref, kc_ref, vc_ref, bias_ref, sink_ref,
                       o_ref, ko_ref, vo_ref, *, t):
    wb = kc_ref.shape[1]
    pad = jnp.zeros((WINDOW - t, SKV_W), F32)

    def seq(s, carry):
        kn = kn_ref[s]
        vn = vn_ref[s]
        kc = kc_ref[s]
        vc = vc_ref[s]
        kband = jnp.concatenate([kc, kn, pad], axis=0).astype(BF16)
        vband = jnp.concatenate([vc, vn, pad], axis=0).astype(BF16)
        vband = jnp.concatenate([vband, jnp.ones_like(vband)], axis=1)
        o_ref[s] = _swa_attend(q_ref[s], kband, vband, bias_ref[...], sink_ref)
        ko_ref[s] = jnp.concatenate([kc[t:wb], kn], axis=0)
        vo_ref[s] = jnp.concatenate([vc[t:wb], vn], axis=0)
        return carry

    lax.fori_loop(0, SWA_SAMPLE_SEQS, seq, 0)


def _swa_sample(qs3, kn3, vn3, kc3, vc3, bias, sink, t):
    nseq = qs3.shape[0]
    ns = SWA_SAMPLE_SEQS
    blk = lambda a: pl.BlockSpec((ns,) + a.shape[1:], lambda i: (i, 0, 0))
    const = lambda a: pl.BlockSpec(a.shape, lambda i: (0, 0))
    return pl.pallas_call(
        functools.partial(_swa_sample_kernel, t=t),
        grid=(nseq // ns,),
        in_specs=[blk(qs3), blk(kn3), blk(vn3), blk(kc3), blk(vc3), const(bias),
                  pl.BlockSpec(memory_space=pltpu.SMEM)],
        out_specs=[blk(qs3), blk(kc3), blk(vc3)],
        out_shape=[jax.ShapeDtypeStruct(qs3.shape, F32),
                   jax.ShapeDtypeStruct(kc3.shape, F32),
                   jax.ShapeDtypeStruct(vc3.shape, F32)],
        compiler_params=pltpu.CompilerParams(
            dimension_semantics=("arbitrary",), vmem_limit_bytes=VMEM_LIMIT),
        name="swa_sample",
    )(qs3, kn3, vn3, kc3, vc3, bias, sink)


def _swa_bias_table(r, sample):
    rows = np.arange(8 * r)
    piece = rows // r
    head = piece // 2 + 4 * (piece % 2)
    i = (rows % r)[:, None]
    slope = (2.0 ** -(head + 1.0))[:, None]
    key = np.arange(2 * WINDOW)[None, :]
    if sample:
        dist = np.where(key < WINDOW, WINDOW + i - key, i - (key - WINDOW))
        valid = (dist >= 0) & (dist < WINDOW) & (key < WINDOW + r)
    else:
        dist = i + WINDOW - key
        valid = (dist >= 0) & (dist < WINDOW)
    return np.where(valid, -slope * dist, -np.inf).astype(np.float32)


POST_ROW_GROUPS = 2


def _post_kernel(og_ref, os_ref, x_ref, p_ref, wog_ref, wos_ref, wpm_ref, wpf_ref, wg_ref,
                 wu_ref, wd_ref, wpo_ref, wpg_ref, wpp_ref, y_ref):
    def rows_chain(rows):
        mix = (_dot(og_ref[rows, :].astype(BF16), wog_ref[...])
               + _dot(os_ref[rows, :].astype(BF16), wos_ref[...]))
        h = x_ref[rows, :] + _rms(mix, wpm_ref[...])
        f = _rms(h, wpf_ref[...]).astype(BF16)
        gate = _dot(f, wg_ref[...])
        up = _dot(f, wu_ref[...])
        act = (gate * _sigmoid(gate) * up).astype(BF16)
        h = h + _rms(_dot(act, wd_ref[...]), wpo_ref[...])
        ple = (_sigmoid(_dot(h.astype(BF16), wpg_ref[...]))
               * _dot(p_ref[rows, :].astype(BF16), wpp_ref[...]))
        return h + ple

    tm = y_ref.shape[0]
    sub = tm // POST_ROW_GROUPS
    y_ref[...] = jnp.concatenate(
        [rows_chain(slice(i * sub, (i + 1) * sub)) for i in range(POST_ROW_GROUPS)], axis=0)


def _post(og, osw, x2d, p2d, weights, tm):
    n = x2d.shape[0]
    row = lambda w: pl.BlockSpec((tm, w), lambda i: (i, 0))
    const = lambda a: pl.BlockSpec(a.shape, lambda i: (0, 0), pipeline_mode=pl.Buffered(1))
    return pl.pallas_call(
        _post_kernel,
        grid=(n // tm,),
        in_specs=[row(V_W), row(SQ_W), row(D_MODEL), row(PLE_DIM)] + [const(w) for w in weights],
        out_specs=row(D_MODEL),
        out_shape=jax.ShapeDtypeStruct((n, D_MODEL), F32),
        compiler_params=pltpu.CompilerParams(
            dimension_semantics=("arbitrary",), vmem_limit_bytes=VMEM_LIMIT),
        name="post",
    )(og, osw, x2d, p2d, *weights)


def _gla_constants(t_rows, seg):
    i = np.arange(t_rows)
    tri = ((i[:, None] >= i[None, :]) & (i[:, None] // seg == i[None, :] // seg))
    e = (np.arange(QK_W)[:, None] // GLA_DK == np.arange(V_W)[None, :] // GLA_DV)
    e64 = (np.arange(V_W)[:, None] // GLA_DV == np.arange(V_W)[None, :] // GLA_DV) / GLA_DV
    return (jnp.asarray(tri, BF16), jnp.asarray(e, BF16), jnp.asarray(e64, BF16))


def _state_from_blocks(st):
    b = st.shape[0]
    st = st.reshape(b, GLA_HEADS, GLA_DK, GLA_HEADS, GLA_DV)
    return jnp.stack([st[:, h, :, h, :] for h in range(GLA_HEADS)], axis=1)


def _state_from_t(st):
    b = st.shape[0]
    return st.reshape(b, GLA_DV, GLA_HEADS, GLA_DK).transpose(0, 2, 3, 1)


def _state_to_t(s):
    b = s.shape[0]
    return s.transpose(0, 3, 1, 2).reshape(b, GLA_DV, QK_W)


def kernel(x_prompt, x_sample, p_prompt, p_sample, state_gla, cache_swa_k, cache_swa_v,
           w_pre_mix, w_in, gla_gate_up, gla_gate_b, gla_norm_w, swa_sinks, w_out,
           w_post_mix, w_pre_ffn, w_gate, w_up, w_down, w_post_ffn, w_ple_gate, w_ple_proj):
    assert w_in.shape[0] == 1, "one trunk layer"
    batch, seq, _ = x_prompt.shape
    dec_batch, dec_seq, _ = x_sample.shape
    assert seq % 512 == 0 and (dec_batch * dec_seq) % 512 == 0
    assert dec_seq == 8, "a sample sequence is one 8-row sublane group"
    assert dec_batch % SWA_SAMPLE_SEQS == 0 and dec_batch % GLA_SAMPLE_SEQS == 0
    assert cache_swa_k.shape[2] == WINDOW
    order = np.asarray(SWA_HEAD_ORDER)

    parts = jnp.split(w_in[0], np.cumsum(IN_SPLITS)[:-1].tolist(), axis=1)
    w_qg, w_kg, w_vg, w_og, w_lr, w_qs, w_ks, w_vs = parts
    w_qs = w_qs.reshape(D_MODEL, SWA_HEADS, SWA_DIM)[:, order, :].reshape(D_MODEL, SQ_W)
    w_lr = jnp.pad(w_lr, ((0, 0), (0, D_IN_P - C_LR - GLA_LOW_RANK)))
    w_in_p = jnp.concatenate([w_qg, w_kg, w_vg, w_og, w_qs, w_ks, w_vs, w_lr], axis=1).astype(BF16)
    gup_p = jnp.pad(gla_gate_up[0], ((0, LR_PAD - GLA_LOW_RANK), (0, 0))).astype(BF16)
    gate_b = gla_gate_b[0].reshape(1, QK_W)
    w_pre = w_pre_mix[0].reshape(1, D_MODEL)
    norm_w = jnp.tile(gla_norm_w[0], GLA_HEADS).reshape(1, V_W)
    w_out_g = w_out[0, :V_W].astype(BF16)
    w_out_s = (w_out[0, V_W:].reshape(SWA_HEADS, SWA_DIM, D_MODEL)[order]
               .reshape(SQ_W, D_MODEL).astype(BF16))
    post_w = (w_out_g, w_out_s, w_post_mix[0].reshape(1, D_MODEL), w_pre_ffn[0].reshape(1, D_MODEL),
              w_gate[0].astype(BF16), w_up[0].astype(BF16), w_down[0].astype(BF16),
              w_post_ffn[0].reshape(1, D_MODEL), w_ple_gate[0].astype(BF16),
              w_ple_proj[0].astype(BF16))

    xp = x_prompt.reshape(batch * seq, D_MODEL)
    qg, kg, vg, og, la, qs, ks, vs, la_min = _in_proj(xp, w_pre, w_in_p, gup_p, gate_b, BF16)
    tri, e, e64 = _gla_constants(GLA_CHUNK, GLA_CHUNK)
    safe = (la_min[:, 0, 0] >= GLA_SAFE_LOG_DECAY).astype(jnp.int32)
    o_g, st_p = _gla_prompt(safe, qg, kg, vg, la, og, norm_w, tri, e, e64, batch, seq, tb=IN_TILE)
    sinks = swa_sinks[0].astype(F32)
    o_s = _swa_prompt(qs, ks, vs, jnp.asarray(_swa_bias_table(WINDOW, sample=False)), sinks,
                      batch, seq)
    y_p = _post(o_g, o_s, xp, p_prompt[0].reshape(batch * seq, PLE_DIM), post_w, tm=512)
    keep = min(WINDOW, seq)
    k_p = ks.reshape(batch, seq, SKV_W)[:, seq - keep:].reshape(batch, keep, SWA_KV_HEADS, SWA_DIM)
    v_p = vs.reshape(batch, seq, SKV_W)[:, seq - keep:].reshape(batch, keep, SWA_KV_HEADS, SWA_DIM)

    n_s = dec_batch * dec_seq
    xs = x_sample.reshape(n_s, D_MODEL)
    qg, kg, vg, og, la, qs, ks, vs, _ = _in_proj(xs, w_pre, w_in_p, gup_p, gate_b, F32)
    tri_s, _, _ = _gla_constants(GLA_SAMPLE_SEQS * dec_seq, dec_seq)
    o_g, st_s = _gla_sample(qg, kg, vg, la, og, norm_w, tri_s, e, e64,
                            _state_to_t(state_gla[0]), dec_seq)
    bias_s = jnp.asarray(_swa_bias_table(dec_seq, sample=True))
    wb = cache_swa_k.shape[2]
    o_s3, k_s, v_s = _swa_sample(
        qs.reshape(dec_batch, dec_seq, SQ_W), ks.reshape(dec_batch, dec_seq, SKV_W),
        vs.reshape(dec_batch, dec_seq, SKV_W), cache_swa_k[0].reshape(dec_batch, wb, SKV_W),
        cache_swa_v[0].reshape(dec_batch, wb, SKV_W), bias_s, sinks, dec_seq)
    y_s = _post(o_g, o_s3.reshape(n_s, SQ_W), xs, p_sample[0].reshape(n_s, PLE_DIM), post_w, tm=512)

    return (y_p.reshape(batch, seq, D_MODEL),
            y_s.reshape(dec_batch, dec_seq, D_MODEL),
            _state_from_blocks(st_p)[None],
            k_p[None], v_p[None],
            _state_from_t(st_s).astype(state_gla.dtype)[None],
            k_s.reshape(dec_batch, wb, SWA_KV_HEADS, SWA_DIM)[None],
            v_s.reshape(dec_batch, wb, SWA_KV_HEADS, SWA_DIM)[None])
```
